```python
import jax, jax.numpy as jnp
from jax import lax
import numpy as np

D_MODEL = 4096
BATCH = 2
SEQ = 8192
DEPTH = 1

N_HEADS = 16
N_KV_HEADS = 4
HEAD_DIM = 128
ATTN_WIDTH = N_HEADS * HEAD_DIM
KV_WIDTH = N_KV_HEADS * HEAD_DIM
IDX_HEADS = 16
IDX_DIM = 64
TOPK_MAX = 256
POOL_WINDOWS = (2, 4, 8, 16)
POOL_WIDTH = D_MODEL // 2
POOL_GROUP = POOL_WIDTH // len(POOL_WINDOWS)
MIX_WIDTH = ATTN_WIDTH + POOL_WIDTH
IN_SIZES = (ATTN_WIDTH, KV_WIDTH, KV_WIDTH, IDX_HEADS * IDX_DIM, IDX_DIM, IDX_HEADS, POOL_WIDTH)
IN_WIDTH = sum(IN_SIZES)
D_FF = -(-8 * D_MODEL // (3 * 256)) * 256
ROPE_THETA = 500000.0
ROT_FRACTION = 4
BLOCK_Q = 128
EPS = 1e-6

kernel_name = "hymba_style_dsa_pool_hybrid"


def rmsnorm(x, g):
    xf = x.astype(jnp.float32)
    y = xf * lax.rsqrt(jnp.mean(xf * xf, axis=-1, keepdims=True) + EPS)
    return (y * g.astype(jnp.float32)).astype(x.dtype)


def rope_tables(pos, rot_dim):
    inv_freq = ROPE_THETA ** (-jnp.arange(0, rot_dim, 2, dtype=jnp.float32) / rot_dim)
    ang = pos.astype(jnp.float32)[:, None] * inv_freq[None, :]
    return jnp.cos(ang), jnp.sin(ang)


def apply_partial_rope(x, cos, sin):
    rot = x.shape[-1] // ROT_FRACTION
    half = rot // 2
    xf = x.astype(jnp.float32)
    x1 = xf[..., :half]
    x2 = xf[..., half:rot]
    c = cos[:, None, :]
    s = sin[:, None, :]
    out = jnp.concatenate([x1 * c - x2 * s, x2 * c + x1 * s, xf[..., rot:]], axis=-1)
    return out.astype(x.dtype)


def dsa_sparse_attention(q, k, v, iq, ik, iw):
    B, S = q.shape[0], q.shape[1]
    topk = min(TOPK_MAX, S // 4)
    nb = S // BLOCK_Q
    group = N_HEADS // N_KV_HEADS
    scale = HEAD_DIM ** -0.5
    idx_scale = (IDX_DIM ** -0.5) * (IDX_HEADS ** -0.5)
    key_pos = jnp.arange(S)

    def to_blocks(a):
        return a.reshape((B, nb, BLOCK_Q) + a.shape[2:]).swapaxes(0, 1)

    qb, iqb, iwb = to_blocks(q), to_blocks(iq), to_blocks(iw)

    def block(args):
        blk, q_blk, iq_blk, iw_blk = args
        qpos = blk * BLOCK_Q + jnp.arange(BLOCK_Q)
        logits = jnp.einsum('bqhd,bsd->bqhs', iq_blk, ik).astype(jnp.float32)
        score = jnp.einsum('bqhs,bqh->bqs', jax.nn.relu(logits),
                           iw_blk.astype(jnp.float32)) * idx_scale
        causal = key_pos[None, :] <= qpos[:, None]
        score = jnp.where(causal[None], score, -jnp.inf)
        _, idx = lax.top_k(score, topk)
        valid = idx <= qpos[None, :, None]
        k_sel = jax.vmap(lambda kb, ib: kb[ib])(k, idx)
        v_sel = jax.vmap(lambda vb, ib: vb[ib])(v, idx)
        qg = q_blk.reshape(B, BLOCK_Q, N_KV_HEADS, group, HEAD_DIM)
        s = jnp.einsum('bqngd,bqknd->bqngk', qg, k_sel).astype(jnp.float32) * scale
        s = jnp.where(valid[:, :, None, None, :], s, -jnp.inf)
        p = jax.nn.softmax(s, axis=-1).astype(v.dtype)
        o = jnp.einsum('bqngk,bqknd->bqngd', p, v_sel)
        return o.reshape(B, BLOCK_Q, N_HEADS * HEAD_DIM)

    out = lax.map(block, (jnp.arange(nb), qb, iqb, iwb))
    return out.swapaxes(0, 1).reshape(B, S, N_HEADS * HEAD_DIM)


def multiscale_pool(u, w_pool, pool_scale):
    B, S, C = u.shape
    G = len(POOL_WINDOWS)
    uf = u.astype(jnp.float32)
    csum = jnp.cumsum(uf, axis=1)
    t = jnp.arange(S)
    pooled = []
    for g, w in enumerate(POOL_WINDOWS):
        cg = csum[..., g * POOL_GROUP:(g + 1) * POOL_GROUP]
        prev = jnp.pad(cg, ((0, 0), (w, 0), (0, 0)))[:, :S]
        cnt = jnp.minimum(t + 1, w).astype(jnp.float32)[None, :, None]
        pooled.append((cg - prev) / cnt)
    pooled = jnp.stack(pooled, axis=2)
    diff = (pooled - uf.reshape(B, S, G, POOL_GROUP)).astype(u.dtype)
    mixed = jnp.einsum('bsgc,gcd->bsgd', diff, w_pool)
    return mixed.reshape(B, S, C) * pool_scale


def setup_inputs(seed: int = 0) -> dict:
    key = jax.random.key(seed)
    ks = jax.random.split(key, 12)
    f32 = jnp.float32

    def dense(k, shape, fan_in):
        return jax.random.normal(k, shape, f32) * (fan_in ** -0.5)

    return {
        "x": jax.random.normal(ks[0], (BATCH, SEQ, D_MODEL), f32),
        "norm_mix_g": 1.0 + 0.02 * jax.random.normal(ks[1], (DEPTH, D_MODEL), f32),
        "w_in": dense(ks[2], (DEPTH, D_MODEL, IN_WIDTH), D_MODEL),
        "w_pool": dense(ks[3], (DEPTH, len(POOL_WINDOWS), POOL_GROUP, POOL_GROUP), POOL_GROUP),
        "pool_scale": 1.0 + 0.02 * jax.random.normal(ks[4], (DEPTH, POOL_WIDTH), f32),
        "w_out": dense(ks[5], (DEPTH, MIX_WIDTH, D_MODEL), MIX_WIDTH),
        "norm_ffn_g": 1.0 + 0.02 * jax.random.normal(ks[6], (DEPTH, D_MODEL), f32),
        "w_gate": dense(ks[7], (DEPTH, D_MODEL, D_FF), D_MODEL),
        "w_up": dense(ks[8], (DEPTH, D_MODEL, D_FF), D_MODEL),
        "w_down": dense(ks[9], (DEPTH, D_FF, D_MODEL), D_FF),
        "norm_final_g": 1.0 + 0.02 * jax.random.normal(ks[10], (D_MODEL,), f32),
    }


def reference(x, norm_mix_g, w_in, w_pool, pool_scale, w_out, norm_ffn_g,
              w_gate, w_up, w_down, norm_final_g):
    B, S, _ = x.shape
    pos = jnp.arange(S)
    cos_a, sin_a = rope_tables(pos, HEAD_DIM // ROT_FRACTION)
    cos_i, sin_i = rope_tables(pos, IDX_DIM // ROT_FRACTION)
    split_points = [int(v) for v in np.cumsum(IN_SIZES)[:-1]]
    for l in range(DEPTH):
        h = rmsnorm(x, norm_mix_g[l])
        z = jnp.einsum('bsd,de->bse', h, w_in[l])
        q, k, v, iq, ik, iw, u = jnp.split(z, split_points, axis=-1)
        q = apply_partial_rope(q.reshape(B, S, N_HEADS, HEAD_DIM), cos_a, sin_a)
        k = apply_partial_rope(k.reshape(B, S, N_KV_HEADS, HEAD_DIM), cos_a, sin_a)
        v = v.reshape(B, S, N_KV_HEADS, HEAD_DIM)
        iq = apply_partial_rope(iq.reshape(B, S, IDX_HEADS, IDX_DIM), cos_i, sin_i)
        ik = apply_partial_rope(ik.reshape(B, S, 1, IDX_DIM), cos_i, sin_i).reshape(B, S, IDX_DIM)
        attn = dsa_sparse_attention(q, k, v, iq, ik, iw)
        pool = multiscale_pool(u, w_pool[l], pool_scale[l])
        mixed = jnp.concatenate([attn, pool], axis=-1)
        x = x + jnp.einsum('bse,ed->bsd', mixed, w_out[l])
        h = rmsnorm(x, norm_ffn_g[l])
        a = jax.nn.silu(jnp.einsum('bsd,df->bsf', h, w_gate[l])) * jnp.einsum('bsd,df->bsf', h, w_up[l])
        x = x + jnp.einsum('bsf,fd->bsd', a, w_down[l])
    return rmsnorm(x, norm_final_g)
```

```python
import functools

import jax
import jax.numpy as jnp
import numpy as np
from jax import lax
from jax.experimental import pallas as pl
from jax.experimental.pallas import tpu as pltpu

N_HEADS = 16
N_KV_HEADS = 4
HEAD_DIM = 128
GROUP = N_HEADS // N_KV_HEADS
ATTN_WIDTH = N_HEADS * HEAD_DIM
KV_WIDTH = N_KV_HEADS * HEAD_DIM
IDX_HEADS = 16
IDX_DIM = 64
IDX_WIDTH = IDX_HEADS * IDX_DIM
TOPK_MAX = 256
POOL_WINDOWS = (2, 4, 8, 16)
MAX_WINDOW = max(POOL_WINDOWS)
ROPE_THETA = 500000.0
ROT_FRACTION = 4
EPS = 1e-6

LANES = 128
FFN_PAD = 1024
VMEM_LIMIT = 56 * 1024 * 1024

TQ = 128
TK = 512
V_ROWS = HEAD_DIM + 16
MASK_NEG = -1e30

INT_MIN = -2 ** 31
NEG_INF_KEY = int(np.array(-np.inf, np.float32).view(np.int32)) ^ 0x7FFFFFFF
MASK_NEG_BITS = int(np.array(MASK_NEG, np.float32).view(np.int32))

BF16 = jnp.bfloat16
F32 = jnp.float32
I32 = jnp.int32


def _params(semantics):
    return pltpu.CompilerParams(dimension_semantics=semantics, vmem_limit_bytes=VMEM_LIMIT)


def _rope(z, tab_ref, half):
    return (z * tab_ref[0]
            + pltpu.roll(z, LANES - half, 1) * tab_ref[1]
            + pltpu.roll(z, half, 1) * tab_ref[2])


def _rmsnorm_bf16(x, g):
    ms = jnp.mean(x * x, axis=-1, keepdims=True)
    return (x * lax.rsqrt(ms + EPS) * g).astype(BF16)


def _inproj_kernel(x_ref, g_ref, wm_ref, ws_ref, ra_ref, ri_ref,
                   qT_ref, kr_ref, vT_ref, iqT_ref, ikp_ref, iwT_ref, u_ref, h_ref):
    j = pl.program_id(1)
    q_scale = HEAD_DIM ** -0.5
    idx_scale = (IDX_DIM ** -0.5) * (IDX_HEADS ** -0.5)
    a_half = HEAD_DIM // ROT_FRACTION // 2
    i_half = IDX_DIM // ROT_FRACTION // 2

    @pl.when(j == 0)
    def _():
        h_ref[...] = _rmsnorm_bf16(x_ref[...], g_ref[...])
        zs = jnp.dot(h_ref[...], ws_ref[...], preferred_element_type=F32)
        ikp_ref[...] = _rope(zs[:, :LANES], ri_ref, i_half).astype(BF16)
        iwT_ref[...] = (zs[:, LANES:] * idx_scale).T

    z = jnp.dot(h_ref[...], wm_ref[...], preferred_element_type=F32)
    n_tiles = z.shape[1] // LANES

    @pl.when(j <= 1)
    def _():
        for t in range(n_tiles):
            zz = _rope(z[:, t * LANES:(t + 1) * LANES], ra_ref, a_half) * q_scale
            qT_ref[t * LANES:(t + 1) * LANES, :] = zz.T.astype(BF16)

    @pl.when(j == 2)
    def _():
        for t in range(N_KV_HEADS):
            zz = _rope(z[:, t * LANES:(t + 1) * LANES], ra_ref, a_half)
            kr_ref[:, t * LANES:(t + 1) * LANES] = zz.astype(BF16)
        for t in range(N_KV_HEADS):
            c0 = KV_WIDTH + t * LANES
            vT_ref[t * LANES:(t + 1) * LANES, :] = z[:, c0:c0 + LANES].T.astype(BF16)

    @pl.when(j == 3)
    def _():
        for t in range(n_tiles):
            zz = _rope(z[:, t * LANES:(t + 1) * LANES], ri_ref, i_half)
            iqT_ref[t * LANES:(t + 1) * LANES, :] = zz.T.astype(BF16)

    @pl.when(j >= 4)
    def _():
        u_ref[...] = z


def _inproj(x2, g, w_main, w_small, rope_a, rope_i, seq):
    T, D = x2.shape
    tm = min(512, seq)
    tn = 1024
    nj = w_main.shape[1] // tn
    pool_w = w_main.shape[1] - 4 * tn
    sb = seq // tm
    out_shape = (
        jax.ShapeDtypeStruct((ATTN_WIDTH, T), BF16),
        jax.ShapeDtypeStruct((T, KV_WIDTH), BF16),
        jax.ShapeDtypeStruct((KV_WIDTH, T), BF16),
        jax.ShapeDtypeStruct((IDX_WIDTH, T), BF16),
        jax.ShapeDtypeStruct((T, LANES), BF16),
        jax.ShapeDtypeStruct((LANES, T), F32),
        jax.ShapeDtypeStruct((T, pool_w), F32),
    )
    return pl.pallas_call(
        _inproj_kernel,
        grid=(T // tm, nj),
        in_specs=[
            pl.BlockSpec((tm, D), lambda i, j: (i, 0), pipeline_mode=pl.Buffered(1)),
            pl.BlockSpec((1, D), lambda i, j: (0, 0)),
            pl.BlockSpec((D, tn), lambda i, j: (0, j)),
            pl.BlockSpec((D, 2 * LANES), lambda i, j: (0, 0), pipeline_mode=pl.Buffered(1)),
            pl.BlockSpec((3, tm, LANES), lambda i, j: (0, i % sb, 0)),
            pl.BlockSpec((3, tm, LANES), lambda i, j: (0, i % sb, 0)),
        ],
        out_specs=(
            pl.BlockSpec((tn, tm), lambda i, j: (jnp.minimum(j, 1), i)),
            pl.BlockSpec((tm, KV_WIDTH), lambda i, j: (i, 0)),
            pl.BlockSpec((KV_WIDTH, tm), lambda i, j: (0, i)),
            pl.BlockSpec((IDX_WIDTH, tm), lambda i, j: (0, i)),
            pl.BlockSpec((tm, LANES), lambda i, j: (i, 0)),
            pl.BlockSpec((LANES, tm), lambda i, j: (0, i)),
            pl.BlockSpec((tm, tn), lambda i, j: (i, jnp.clip(j - 4, 0, pool_w // tn - 1))),
        ),
        out_shape=out_shape,
        scratch_shapes=[pltpu.VMEM((tm, D), BF16)],
        compiler_params=_params(("arbitrary", "arbitrary")),
        name="inproj",
    )(x2, g, w_main, w_small, rope_a, rope_i)


def _pool_kernel(u_ref, wp_ref, ps_ref, o_ref, ext_ref):
    s = pl.program_id(1)
    ts = u_ref.shape[0]
    cg = wp_ref.shape[1]

    @pl.when(s == 0)
    def _():
        ext_ref[0:MAX_WINDOW, :] = jnp.zeros((MAX_WINDOW, ext_ref.shape[1]), F32)

    @pl.when(s > 0)
    def _():
        ext_ref[0:MAX_WINDOW, :] = ext_ref[ts:ts + MAX_WINDOW, :]

    ext_ref[MAX_WINDOW:MAX_WINDOW + ts, :] = u_ref[...]
    t = s * ts + lax.broadcasted_iota(I32, (ts, 1), 0)
    for g, w in enumerate(POOL_WINDOWS):
        c0 = g * cg
        cur = ext_ref[MAX_WINDOW:MAX_WINDOW + ts, c0:c0 + cg]
        acc = cur
        for back in range(1, w):
            acc = acc + ext_ref[MAX_WINDOW - back:MAX_WINDOW - back + ts, c0:c0 + cg]
        cnt = jnp.minimum(t + 1, w).astype(F32)
        diff = (acc / cnt - cur).astype(BF16)
        mixed = jnp.dot(diff, wp_ref[g], preferred_element_type=F32)
        o_ref[:, c0:c0 + cg] = (mixed * ps_ref[:, c0:c0 + cg]).astype(BF16)


def _pool(u, w_pool, pool_scale, batch, seq):
    T, C = u.shape
    ts = min(512, seq)
    ns = seq // ts
    G, cg, _ = w_pool.shape
    return pl.pallas_call(
        _pool_kernel,
        grid=(batch, ns),
        in_specs=[
            pl.BlockSpec((ts, C), lambda b, s: (b * ns + s, 0)),
            pl.BlockSpec((G, cg, cg), lambda b, s: (0, 0, 0)),
            pl.BlockSpec((1, C), lambda b, s: (0, 0)),
        ],
        out_specs=pl.BlockSpec((ts, C), lambda b, s: (b * ns + s, 0)),
        out_shape=jax.ShapeDtypeStruct((T, C), BF16),
        scratch_shapes=[pltpu.VMEM((ts + MAX_WINDOW, C), F32)],
        compiler_params=_params(("arbitrary", "arbitrary")),
        name="pool",
    )(u, w_pool, pool_scale)


def _attn_kernel(qT_ref, iqT_ref, iwT_ref, kr_ref, ikp_ref, vT_ref, o_ref,
                 key_ref, vte_ref, cut_ref, *, seq, topk):
    i = pl.program_id(1)
    n_tiles = (i * TQ + TQ + TK - 1) // TK
    row_bits = (seq - 1).bit_length()

    @pl.when(i == 0)
    def _():
        for n in range(N_KV_HEADS):
            vte_ref[n, 0:HEAD_DIM, :] = vT_ref[n * HEAD_DIM:(n + 1) * HEAD_DIM, :]
            vte_ref[n, HEAD_DIM:V_ROWS, :] = jnp.ones((V_ROWS - HEAD_DIM, seq), BF16)

    qpos = i * TQ + lax.broadcasted_iota(I32, (1, TQ), 1)

    def tile_rows(t):
        r0 = pl.multiple_of(t * TK, TK)
        return r0, r0 + lax.broadcasted_iota(I32, (TK, TQ), 0)

    iq = iqT_ref[...]
    idx_rhs = jnp.concatenate(
        [iq[h * IDX_DIM:(h + 1) * IDX_DIM, :] for h in range(IDX_HEADS)], axis=1)
    idx_rhs = jnp.concatenate([idx_rhs, jnp.zeros_like(idx_rhs)], axis=0)
    iw = iwT_ref[...]

    def score_body(t, carry):
        r0, rows = tile_rows(t)
        logits = jnp.dot(ikp_ref[pl.ds(r0, TK), :], idx_rhs, preferred_element_type=F32)
        sc = jnp.zeros((TK, TQ), F32)
        for h in range(IDX_HEADS):
            sc = sc + jnp.maximum(logits[:, h * TQ:(h + 1) * TQ], 0.0) * iw[h:h + 1, :]
        sc = jnp.where(rows <= qpos, sc + 0.0, -jnp.inf)
        bits = pltpu.bitcast(sc, I32)
        key_ref[pl.ds(r0, TK), :] = jnp.where(bits < 0, bits ^ 0x7FFFFFFF, bits)
        return carry

    lax.fori_loop(0, n_tiles, score_body, 0)

    def count(pred):
        def body(t, c8):
            r0, rows = tile_rows(t)
            m = pred(key_ref[pl.ds(r0, TK), :], rows).astype(I32)
            return c8 + jnp.sum(m.reshape(TK // 8, 8, TQ), axis=0)
        c8 = lax.fori_loop(0, n_tiles, body, jnp.zeros((8, TQ), I32))
        return jnp.sum(c8, axis=0, keepdims=True)

    def value_bit(p, lo):
        cand = lo + jnp.left_shift(jnp.int32(1), 31 - p)
        cnt = count(lambda k, rows: k >= cand)
        return jnp.where(cnt >= topk, cand, lo)

    tau = lax.fori_loop(0, 32, value_bit, jnp.full((1, TQ), INT_MIN, I32))
    c_ge = count(lambda k, rows: k >= tau)
    c_gt = count(lambda k, rows: k > tau)
    need = topk - c_gt

    cut_ref[...] = jnp.full((1, TQ), seq, I32)
    tied = jnp.logical_and(c_ge > topk, tau > NEG_INF_KEY)

    @pl.when(jnp.max(tied.astype(I32)) > 0)
    def _():
        def row_bit(p, x):
            cand = x + jnp.left_shift(jnp.int32(1), row_bits - 1 - p)
            cnt = count(lambda k, rows: jnp.logical_and(k == tau, rows < cand))
            return jnp.where(cnt < need, cand, x)
        cut_ref[...] = lax.fori_loop(0, row_bits, row_bit, jnp.zeros((1, TQ), I32))

    cut = cut_ref[...]

    def mask_body(t, carry):
        r0, rows = tile_rows(t)
        k = key_ref[pl.ds(r0, TK), :]
        sel = jnp.logical_or(k > tau, jnp.logical_and(k == tau, rows <= cut))
        sel = jnp.logical_and(sel, rows <= qpos)
        key_ref[pl.ds(r0, TK), :] = jnp.where(sel, 0, MASK_NEG_BITS).astype(I32)
        return carry

    lax.fori_loop(0, n_tiles, mask_body, 0)

    for n in range(N_KV_HEADS):
        qTn = jnp.concatenate(
            [qT_ref[(GROUP * n + g) * HEAD_DIM:(GROUP * n + g + 1) * HEAD_DIM, :]
             for g in range(GROUP)], axis=1)

        def attn_body(t, carry, n=n, qTn=qTn):
            m, acc = carry
            r0 = pl.multiple_of(t * TK, TK)
            s = jnp.dot(kr_ref[pl.ds(r0, TK), n * HEAD_DIM:(n + 1) * HEAD_DIM], qTn,
                        preferred_element_type=F32)
            bias = pltpu.bitcast(key_ref[pl.ds(r0, TK), :], F32)
            s = s + jnp.concatenate([bias] * GROUP, axis=1)
            m_new = jnp.maximum(m, jnp.max(s, axis=0, keepdims=True))
            alpha = jnp.exp(m - m_new)
            p = jnp.exp(s - m_new).astype(BF16)
            pv = jnp.dot(vte_ref[n, :, pl.ds(r0, TK)], p, preferred_element_type=F32)
            return m_new, acc * alpha + pv

        _, acc = lax.fori_loop(
            0, n_tiles, attn_body,
            (jnp.full((1, GROUP * TQ), MASK_NEG, F32), jnp.zeros((V_ROWS, GROUP * TQ), F32)))
        o = acc[0:HEAD_DIM, :] / acc[HEAD_DIM:HEAD_DIM + 1, :]
        for g in range(GROUP):
            c0 = (GROUP * n + g) * HEAD_DIM
            o_ref[:, c0:c0 + HEAD_DIM] = o[:, g * TQ:(g + 1) * TQ].T.astype(BF16)


def _attention(qT, iqT, iwT, kr, ikp, vT, batch, seq):
    T = kr.shape[0]
    nq = seq // TQ
    topk = min(TOPK_MAX, seq // 4)
    once = pl.Buffered(1)
    return pl.pallas_call(
        functools.partial(_attn_kernel, seq=seq, topk=topk),
        grid=(batch, nq),
        in_specs=[
            pl.BlockSpec((ATTN_WIDTH, TQ), lambda b, i: (0, b * nq + i)),
            pl.BlockSpec((IDX_WIDTH, TQ), lambda b, i: (0, b * nq + i)),
            pl.BlockSpec((LANES, TQ), lambda b, i: (0, b * nq + i)),
            pl.BlockSpec((seq, KV_WIDTH), lambda b, i: (b, 0), pipeline_mode=once),
            pl.BlockSpec((seq, LANES), lambda b, i: (b, 0), pipeline_mode=once),
            pl.BlockSpec((KV_WIDTH, seq), lambda b, i: (0, b), pipeline_mode=once),
        ],
        out_specs=pl.BlockSpec((TQ, ATTN_WIDTH), lambda b, i: (b * nq + i, 0)),
        out_shape=jax.ShapeDtypeStruct((T, ATTN_WIDTH), BF16),
        scratch_shapes=[
            pltpu.VMEM((seq, TQ), I32),
            pltpu.VMEM((N_KV_HEADS, V_ROWS, seq), BF16),
            pltpu.VMEM((1, TQ), I32),
        ],
        compiler_params=_params(("arbitrary", "arbitrary")),
        name="sparse_attn",
    )(qT, iqT, iwT, kr, ikp, vT)


def _outproj_kernel(a_ref, p_ref, wa_ref, wp_ref, x_ref, o_ref):
    o_ref[...] = (x_ref[...]
                  + jnp.dot(a_ref[...], wa_ref[...], preferred_element_type=F32)
                  + jnp.dot(p_ref[...], wp_ref[...], preferred_element_type=F32))


def _outproj(attn, pool, w_out, x2):
    T, D = x2.shape
    tm, tn = 512, 1024
    ka, kp = attn.shape[1], pool.shape[1]
    assert ka == kp
    return pl.pallas_call(
        _outproj_kernel,
        grid=(T // tm, D // tn),
        in_specs=[
            pl.BlockSpec((tm, ka), lambda i, j: (i, 0)),
            pl.BlockSpec((tm, kp), lambda i, j: (i, 0)),
            pl.BlockSpec((ka, tn), lambda i, j: (0, j)),
            pl.BlockSpec((kp, tn), lambda i, j: (1, j)),
            pl.BlockSpec((tm, tn), lambda i, j: (i, j)),
        ],
        out_specs=pl.BlockSpec((tm, tn), lambda i, j: (i, j)),
        out_shape=jax.ShapeDtypeStruct((T, D), F32),
        compiler_params=_params(("parallel", "parallel")),
        name="outproj",
    )(attn, pool, w_out, w_out, x2)


def _gateup_kernel(x_ref, g_ref, wg_ref, wu_ref, a_ref, h_ref):
    @pl.when(pl.program_id(1) == 0)
    def _():
        h_ref[...] = _rmsnorm_bf16(x_ref[...], g_ref[...])

    h = h_ref[...]
    gate = jnp.dot(h, wg_ref[...], preferred_element_type=F32)
    up = jnp.dot(h, wu_ref[...], preferred_element_type=F32)
    a_ref[...] = (gate * (1.0 / (1.0 + jnp.exp(-gate))) * up).astype(BF16)


def _gateup(x1, g, w_gate, w_up):
    T, D = x1.shape
    F = w_gate.shape[1]
    tm, tn = 512, 512
    return pl.pallas_call(
        _gateup_kernel,
        grid=(T // tm, F // tn),
        in_specs=[
            pl.BlockSpec((tm, D), lambda i, j: (i, 0)),
            pl.BlockSpec((1, D), lambda i, j: (0, 0)),
            pl.BlockSpec((D, tn), lambda i, j: (0, j)),
            pl.BlockSpec((D, tn), lambda i, j: (0, j)),
        ],
        out_specs=pl.BlockSpec((tm, tn), lambda i, j: (i, j)),
        out_shape=jax.ShapeDtypeStruct((T, F), BF16),
        scratch_shapes=[pltpu.VMEM((tm, D), BF16)],
        compiler_params=_params(("arbitrary", "arbitrary")),
        name="ffn_gateup",
    )(x1, g, w_gate, w_up)


def _down_kernel(a_ref, w_ref, x_ref, o_ref, acc_ref):
    k = pl.program_id(2)

    @pl.when(k == 0)
    def _():
        acc_ref[...] = x_ref[...]

    acc_ref[...] += jnp.dot(a_ref[...], w_ref[...], preferred_element_type=F32)

    @pl.when(k == pl.num_programs(2) - 1)
    def _():
        o_ref[...] = acc_ref[...]


def _down(a, w_down, x1):
    T, D = x1.shape
    F = a.shape[1]
    tm, tn = 512, 1024
    tk = F // 4 if (F // 4) % LANES == 0 else F
    return pl.pallas_call(
        _down_kernel,
        grid=(T // tm, D // tn, F // tk),
        in_specs=[
            pl.BlockSpec((tm, tk), lambda i, j, k: (i, k)),
            pl.BlockSpec((tk, tn), lambda i, j, k: (k, j)),
            pl.BlockSpec((tm, tn), lambda i, j, k: (i, j)),
        ],
        out_specs=pl.BlockSpec((tm, tn), lambda i, j, k: (i, j)),
        out_shape=jax.ShapeDtypeStruct((T, D), F32),
        scratch_shapes=[pltpu.VMEM((tm, tn), F32)],
        compiler_params=_params(("parallel", "parallel", "arbitrary")),
        name="ffn_down",
    )(a, w_down, x1)


def _final_norm_kernel(x_ref, g_ref, o_ref):
    x = x_ref[...]
    ms = jnp.mean(x * x, axis=-1, keepdims=True)
    o_ref[...] = x * lax.rsqrt(ms + EPS) * g_ref[...]


def _final_norm(x2, g):
    T, D = x2.shape
    tm = 256
    return pl.pallas_call(
        _final_norm_kernel,
        grid=(T // tm,),
        in_specs=[pl.BlockSpec((tm, D), lambda i: (i, 0)),
                  pl.BlockSpec((1, D), lambda i: (0, 0))],
        out_specs=pl.BlockSpec((tm, D), lambda i: (i, 0)),
        out_shape=jax.ShapeDtypeStruct((T, D), F32),
        compiler_params=_params(("parallel",)),
        name="final_norm",
    )(x2, g)


def _rope_table(seq, head_dim):
    rot = head_dim // ROT_FRACTION
    half = rot // 2
    inv_freq = ROPE_THETA ** (-jnp.arange(0, rot, 2, dtype=F32) / rot)
    ang = jnp.arange(seq).astype(F32)[:, None] * inv_freq[None, :]
    cos, sin = jnp.cos(ang), jnp.sin(ang)
    zeros = lambda n: jnp.zeros((seq, n), F32)
    c = jnp.concatenate([cos, cos, jnp.ones((seq, head_dim - rot), F32)], axis=1)
    sa = jnp.concatenate([-sin, zeros(head_dim - half)], axis=1)
    sb = jnp.concatenate([zeros(half), sin, zeros(head_dim - rot)], axis=1)
    reps = LANES // head_dim
    return jnp.stack([jnp.tile(t, (1, reps)) for t in (c, sa, sb)])


def kernel(x, norm_mix_g, w_in, w_pool, pool_scale, w_out, norm_ffn_g, w_gate, w_up, w_down,
           norm_final_g):
    B, S, D = x.shape
    T = B * S
    depth = w_in.shape[0]
    d_ff = w_gate.shape[2]
    pool_w = pool_scale.shape[1]
    f_pad = -(-d_ff // FFN_PAD) * FFN_PAD
    small0 = ATTN_WIDTH + 2 * KV_WIDTH + IDX_WIDTH
    small1 = small0 + IDX_DIM + IDX_HEADS

    rope_a = _rope_table(S, HEAD_DIM)
    rope_i = _rope_table(S, IDX_DIM)
    x2 = x.reshape(T, D)
    for l in range(depth):
        wl = w_in[l]
        w_main = jnp.concatenate([wl[:, :small0], wl[:, small1:]], axis=1).astype(BF16)
        w_small = jnp.concatenate(
            [wl[:, small0:small0 + IDX_DIM], jnp.zeros((D, LANES - IDX_DIM), F32),
             wl[:, small0 + IDX_DIM:small1], jnp.zeros((D, LANES - IDX_HEADS), F32)],
            axis=1).astype(BF16)
        pad_f = lambda w, axis: jnp.pad(
            w.astype(BF16), [(0, f_pad - d_ff) if a == axis else (0, 0) for a in range(2)])

        qT, kr, vT, iqT, ikp, iwT, u = _inproj(
            x2, norm_mix_g[l][None], w_main, w_small, rope_a, rope_i, S)
        pool = _pool(u, w_pool[l].astype(BF16), pool_scale[l][None], B, S)
        attn = _attention(qT, iqT, iwT, kr, ikp, vT, B, S)
        x2 = _outproj(attn, pool, w_out[l].astype(BF16), x2)
        a = _gateup(x2, norm_ffn_g[l][None], pad_f(w_gate[l], 1), pad_f(w_up[l], 1))
        x2 = _down(a, pad_f(w_down[l], 0), x2)
    return _final_norm(x2, norm_final_g[None]).reshape(B, S, D)
```

```python
import functools

import jax
import jax.numpy as jnp
import numpy as np
from jax import lax
from jax.experimental import pallas as pl
from jax.experimental.pallas import tpu as pltpu

N_HEADS = 16
N_KV_HEADS = 4
HEAD_DIM = 128
GROUP = N_HEADS // N_KV_HEADS
ATTN_WIDTH = N_HEADS * HEAD_DIM
KV_WIDTH = N_KV_HEADS * HEAD_DIM
IDX_HEADS = 16
IDX_DIM = 64
IDX_WIDTH = IDX_HEADS * IDX_DIM
TOPK_MAX = 256
POOL_WINDOWS = (2, 4, 8, 16)
MAX_WINDOW = max(POOL_WINDOWS)
ROPE_THETA = 500000.0
ROT_FRACTION = 4
EPS = 1e-6

LANES = 128
FFN_PAD = 1024
VMEM_LIMIT = 56 * 1024 * 1024

TQ = 128
TK = 512
V_ROWS = HEAD_DIM + 16
MASK_NEG = -1e30

I16_MIN = -2 ** 15
PACK = 16
LOG2_E = 1.4426950408889634
NEG_INF_KEY = int(np.array(-np.inf, np.float32).view(np.int32)) ^ 0x7FFFFFFF
MASK_NEG_BITS = int(np.array(MASK_NEG, np.float32).view(np.int32))

BF16 = jnp.bfloat16
F32 = jnp.float32
I32 = jnp.int32
I16 = jnp.int16


def _params(semantics):
    return pltpu.CompilerParams(dimension_semantics=semantics, vmem_limit_bytes=VMEM_LIMIT)


def _rope(z, tab_ref, half):
    return (z * tab_ref[0]
            + pltpu.roll(z, LANES - half, 1) * tab_ref[1]
            + pltpu.roll(z, half, 1) * tab_ref[2])


def _rmsnorm_bf16(x, g):
    ms = jnp.mean(x * x, axis=-1, keepdims=True)
    return (x * lax.rsqrt(ms + EPS) * g).astype(BF16)


def _inproj_kernel(x_ref, g_ref, wm_ref, ws_ref, ra_ref, ri_ref,
                   qT_ref, kr_ref, vT_ref, iqT_ref, ikp_ref, iwT_ref, u_ref, h_ref):
    j = pl.program_id(1)
    q_scale = HEAD_DIM ** -0.5 * LOG2_E
    idx_scale = (IDX_DIM ** -0.5) * (IDX_HEADS ** -0.5)
    a_half = HEAD_DIM // ROT_FRACTION // 2
    i_half = IDX_DIM // ROT_FRACTION // 2

    @pl.when(j == 0)
    def _():
        h_ref[...] = _rmsnorm_bf16(x_ref[...], g_ref[...])
        zs = jnp.dot(h_ref[...], ws_ref[...], preferred_element_type=F32)
        ikp_ref[...] = _rope(zs[:, :LANES], ri_ref, i_half).astype(BF16)
        iwT_ref[...] = (zs[:, LANES:] * idx_scale).T

    z = jnp.dot(h_ref[...], wm_ref[...], preferred_element_type=F32)
    n_tiles = z.shape[1] // LANES

    @pl.when(j <= 1)
    def _():
        for t in range(n_tiles):
            zz = _rope(z[:, t * LANES:(t + 1) * LANES], ra_ref, a_half) * q_scale
            qT_ref[t * LANES:(t + 1) * LANES, :] = zz.T.astype(BF16)

    @pl.when(j == 2)
    def _():
        for t in range(N_KV_HEADS):
            zz = _rope(z[:, t * LANES:(t + 1) * LANES], ra_ref, a_half)
            kr_ref[:, t * LANES:(t + 1) * LANES] = zz.astype(BF16)
        for t in range(N_KV_HEADS):
            c0 = KV_WIDTH + t * LANES
            vT_ref[t * LANES:(t + 1) * LANES, :] = z[:, c0:c0 + LANES].T.astype(BF16)

    @pl.when(j == 3)
    def _():
        for t in range(n_tiles):
            zz = _rope(z[:, t * LANES:(t + 1) * LANES], ri_ref, i_half)
            iqT_ref[t * LANES:(t + 1) * LANES, :] = zz.T.astype(BF16)

    @pl.when(j >= 4)
    def _():
        u_ref[...] = z


def _inproj(x2, g, w_main, w_small, rope_a, rope_i, seq):
    T, D = x2.shape
    tm = min(512, seq)
    tn = 1024
    nj = w_main.shape[1] // tn
    pool_w = w_main.shape[1] - 4 * tn
    sb = seq // tm
    out_shape = (
        jax.ShapeDtypeStruct((ATTN_WIDTH, T), BF16),
        jax.ShapeDtypeStruct((T, KV_WIDTH), BF16),
        jax.ShapeDtypeStruct((KV_WIDTH, T), BF16),
        jax.ShapeDtypeStruct((IDX_WIDTH, T), BF16),
        jax.ShapeDtypeStruct((T, LANES), BF16),
        jax.ShapeDtypeStruct((LANES, T), F32),
        jax.ShapeDtypeStruct((T, pool_w), F32),
    )
    return pl.pallas_call(
        _inproj_kernel,
        grid=(T // tm, nj),
        in_specs=[
            pl.BlockSpec((tm, D), lambda i, j: (i, 0), pipeline_mode=pl.Buffered(1)),
            pl.BlockSpec((1, D), lambda i, j: (0, 0)),
            pl.BlockSpec((D, tn), lambda i, j: (0, j)),
            pl.BlockSpec((D, 2 * LANES), lambda i, j: (0, 0), pipeline_mode=pl.Buffered(1)),
            pl.BlockSpec((3, tm, LANES), lambda i, j: (0, i % sb, 0)),
            pl.BlockSpec((3, tm, LANES), lambda i, j: (0, i % sb, 0)),
        ],
        out_specs=(
            pl.BlockSpec((tn, tm), lambda i, j: (jnp.minimum(j, 1), i)),
            pl.BlockSpec((tm, KV_WIDTH), lambda i, j: (i, 0)),
            pl.BlockSpec((KV_WIDTH, tm), lambda i, j: (0, i)),
            pl.BlockSpec((IDX_WIDTH, tm), lambda i, j: (0, i)),
            pl.BlockSpec((tm, LANES), lambda i, j: (i, 0)),
            pl.BlockSpec((LANES, tm), lambda i, j: (0, i)),
            pl.BlockSpec((tm, tn), lambda i, j: (i, jnp.clip(j - 4, 0, pool_w // tn - 1))),
        ),
        out_shape=out_shape,
        scratch_shapes=[pltpu.VMEM((tm, D), BF16)],
        compiler_params=_params(("arbitrary", "arbitrary")),
        name="inproj",
    )(x2, g, w_main, w_small, rope_a, rope_i)


def _pool_kernel(u_ref, wp_ref, ps_ref, o_ref, ext_ref):
    s = pl.program_id(1)
    ts = u_ref.shape[0]
    cg = wp_ref.shape[1]

    @pl.when(s == 0)
    def _():
        ext_ref[0:MAX_WINDOW, :] = jnp.zeros((MAX_WINDOW, ext_ref.shape[1]), F32)

    @pl.when(s > 0)
    def _():
        ext_ref[0:MAX_WINDOW, :] = ext_ref[ts:ts + MAX_WINDOW, :]

    ext_ref[MAX_WINDOW:MAX_WINDOW + ts, :] = u_ref[...]
    t = s * ts + lax.broadcasted_iota(I32, (ts, 1), 0)
    for g, w in enumerate(POOL_WINDOWS):
        c0 = g * cg
        cur = ext_ref[MAX_WINDOW:MAX_WINDOW + ts, c0:c0 + cg]
        acc = cur
        for back in range(1, w):
            acc = acc + ext_ref[MAX_WINDOW - back:MAX_WINDOW - back + ts, c0:c0 + cg]
        cnt = jnp.minimum(t + 1, w).astype(F32)
        diff = (acc / cnt - cur).astype(BF16)
        mixed = jnp.dot(diff, wp_ref[g], preferred_element_type=F32)
        o_ref[:, c0:c0 + cg] = (mixed * ps_ref[:, c0:c0 + cg]).astype(BF16)


def _pool(u, w_pool, pool_scale, batch, seq):
    T, C = u.shape
    ts = min(512, seq)
    ns = seq // ts
    G, cg, _ = w_pool.shape
    return pl.pallas_call(
        _pool_kernel,
        grid=(batch, ns),
        in_specs=[
            pl.BlockSpec((ts, C), lambda b, s: (b * ns + s, 0)),
            pl.BlockSpec((G, cg, cg), lambda b, s: (0, 0, 0)),
            pl.BlockSpec((1, C), lambda b, s: (0, 0)),
        ],
        out_specs=pl.BlockSpec((ts, C), lambda b, s: (b * ns + s, 0)),
        out_shape=jax.ShapeDtypeStruct((T, C), BF16),
        scratch_shapes=[pltpu.VMEM((ts + MAX_WINDOW, C), F32)],
        compiler_params=_params(("arbitrary", "arbitrary")),
        name="pool",
    )(u, w_pool, pool_scale)


def _attn_kernel(qT_ref, iqT_ref, iwT_ref, kr_ref, ikp_ref, vT_ref, o_ref,
                 key_ref, hi_ref, lo_ref, vte_ref, cut_ref, qg_ref, m_ref, acc_ref,
                 s0_ref, s1_ref, mt0_ref, mt1_ref, *, seq, topk):
    i = pl.program_id(1)
    n_tiles = (i * TQ + TQ + TK - 1) // TK
    row_bits = (seq - 1).bit_length()

    @pl.when(i == 0)
    def _():
        for n in range(N_KV_HEADS):
            vte_ref[n, 0:HEAD_DIM, :] = vT_ref[n * HEAD_DIM:(n + 1) * HEAD_DIM, :]
            vte_ref[n, HEAD_DIM:V_ROWS, :] = jnp.ones((V_ROWS - HEAD_DIM, seq), BF16)

    qpos = i * TQ + lax.broadcasted_iota(I32, (1, TQ), 1)

    def tile_rows(t):
        r0 = pl.multiple_of(t * TK, TK)
        return r0, r0 + lax.broadcasted_iota(I32, (TK, TQ), 0)

    iq = iqT_ref[...]
    idx_rhs = jnp.concatenate(
        [iq[h * IDX_DIM:(h + 1) * IDX_DIM, :] for h in range(IDX_HEADS)], axis=1)
    idx_rhs = jnp.concatenate([idx_rhs, jnp.zeros_like(idx_rhs)], axis=0)
    iw = iwT_ref[...]

    def score_body(t, carry):
        r0, rows = tile_rows(t)
        logits = jnp.dot(ikp_ref[pl.ds(r0, TK), :], idx_rhs, preferred_element_type=F32)
        sc = jnp.zeros((TK, TQ), F32)
        for h in range(IDX_HEADS):
            sc = sc + jnp.maximum(logits[:, h * TQ:(h + 1) * TQ], 0.0) * iw[h:h + 1, :]
        sc = jnp.where(rows <= qpos, sc + 0.0, -jnp.inf)
        bits = pltpu.bitcast(sc, I32)
        key = jnp.where(bits < 0, bits ^ 0x7FFFFFFF, bits)
        key_ref[pl.ds(r0, TK), :] = key
        hi_ref[pl.ds(r0, TK), :] = (key >> 16).astype(I16)
        lo_ref[pl.ds(r0, TK), :] = (((key ^ 0x8000) << 16) >> 16).astype(I16)
        return carry

    lax.fori_loop(0, n_tiles, score_body, 0)

    def rows16(v):
        return jnp.broadcast_to(v, (PACK, TQ)).astype(I16)

    def count16(ref, cand, op):
        c16 = rows16(cand)

        def body(t, accs):
            v = ref[pl.ds(pl.multiple_of(t * TK, TK), TK), :]
            accs = list(accs)
            for r in range(TK // PACK):
                a = r % len(accs)
                hit = op(v[r * PACK:(r + 1) * PACK, :], c16)
                accs[a] = accs[a] + jnp.where(hit, jnp.int16(1), jnp.int16(0))
            return tuple(accs)

        zero = jnp.zeros((PACK, TQ), I16)
        accs = lax.fori_loop(0, n_tiles, body, (zero,) * 4)
        total = sum(a.astype(I32) for a in accs)
        return jnp.sum(total, axis=0, keepdims=True)

    ge = lambda v, c: v >= c
    gt = lambda v, c: v > c

    def half_search(ref, base):
        def bit(p, lo):
            cand = lo + jnp.left_shift(jnp.int32(1), 15 - p)
            cnt = base + count16(ref, cand, ge)
            return jnp.where(cnt >= topk, cand, lo)
        return lax.fori_loop(0, 16, bit, jnp.full((1, TQ), I16_MIN, I32))

    tau_hi = half_search(hi_ref, 0)
    c_gt_hi = count16(hi_ref, tau_hi, gt)
    tau_hi16 = rows16(tau_hi)

    def narrow_body(t, carry):
        sl = pl.ds(pl.multiple_of(t * TK, TK), TK)
        hi, lo = hi_ref[sl, :], lo_ref[sl, :]
        lo_ref[sl, :] = jnp.concatenate(
            [jnp.where(hi[r * PACK:(r + 1) * PACK, :] == tau_hi16,
                       lo[r * PACK:(r + 1) * PACK, :], jnp.int16(I16_MIN))
             for r in range(TK // PACK)], axis=0)
        return carry

    lax.fori_loop(0, n_tiles, narrow_body, 0)
    tau_lo = half_search(lo_ref, c_gt_hi)
    tau = tau_hi * 65536 + (tau_lo - I16_MIN)

    def count(pred):
        def body(t, c8):
            r0, rows = tile_rows(t)
            m = pred(key_ref[pl.ds(r0, TK), :], rows).astype(I32)
            return c8 + jnp.sum(m.reshape(TK // 8, 8, TQ), axis=0)
        c8 = lax.fori_loop(0, n_tiles, body, jnp.zeros((8, TQ), I32))
        return jnp.sum(c8, axis=0, keepdims=True)

    c_ge = count(lambda k, rows: k >= tau)
    c_gt = count(lambda k, rows: k > tau)
    need = topk - c_gt

    cut_ref[...] = jnp.full((1, TQ), seq, I32)
    tied = jnp.logical_and(c_ge > topk, tau > NEG_INF_KEY)

    @pl.when(jnp.max(tied.astype(I32)) > 0)
    def _():
        def row_bit(p, x):
            cand = x + jnp.left_shift(jnp.int32(1), row_bits - 1 - p)
            cnt = count(lambda k, rows: jnp.logical_and(k == tau, rows < cand))
            return jnp.where(cnt < need, cand, x)
        cut_ref[...] = lax.fori_loop(0, row_bits, row_bit, jnp.zeros((1, TQ), I32))

    cut = cut_ref[...]

    def mask_body(t, carry):
        r0, rows = tile_rows(t)
        k = key_ref[pl.ds(r0, TK), :]
        sel = jnp.logical_or(k > tau, jnp.logical_and(k == tau, rows <= cut))
        sel = jnp.logical_and(sel, rows <= qpos)
        key_ref[pl.ds(r0, TK), :] = jnp.where(sel, 0, MASK_NEG_BITS).astype(I32)
        return carry

    lax.fori_loop(0, n_tiles, mask_body, 0)

    for n in range(N_KV_HEADS):
        qg_ref[n] = jnp.concatenate(
            [qT_ref[(GROUP * n + g) * HEAD_DIM:(GROUP * n + g + 1) * HEAD_DIM, :]
             for g in range(GROUP)], axis=1)
        m_ref[n] = jnp.full((1, GROUP * TQ), MASK_NEG, F32)
        acc_ref[n] = jnp.zeros((V_ROWS, GROUP * TQ), F32)

    def score_stage(t, s_ref, mt_ref):
        r0 = pl.multiple_of(jnp.minimum(t, n_tiles - 1) * TK, TK)
        bias = pltpu.bitcast(key_ref[pl.ds(r0, TK), :], F32)
        bias = jnp.where(t < n_tiles, bias, MASK_NEG)
        bias = jnp.concatenate([bias] * GROUP, axis=1)
        for n in range(N_KV_HEADS):
            s = jnp.dot(kr_ref[pl.ds(r0, TK), n * HEAD_DIM:(n + 1) * HEAD_DIM], qg_ref[n],
                        preferred_element_type=F32) + bias
            s_ref[n] = s
            mt_ref[n] = jnp.max(s, axis=0, keepdims=True)

    def value_stage(t, s_ref, mt_ref):
        r0 = pl.multiple_of(jnp.minimum(t, n_tiles - 1) * TK, TK)
        for n in range(N_KV_HEADS):
            m = m_ref[n]
            m_new = jnp.maximum(m, mt_ref[n])
            p = jnp.exp2(s_ref[n] - m_new).astype(BF16)
            pv = jnp.dot(vte_ref[n, :, pl.ds(r0, TK)], p, preferred_element_type=F32)
            acc_ref[n] = acc_ref[n] * jnp.exp2(m - m_new) + pv
            m_ref[n] = m_new

    score_stage(0, s0_ref, mt0_ref)

    def attn_body(tt, carry):
        t = 2 * tt
        score_stage(t + 1, s1_ref, mt1_ref)
        value_stage(t, s0_ref, mt0_ref)
        score_stage(t + 2, s0_ref, mt0_ref)
        value_stage(t + 1, s1_ref, mt1_ref)
        return carry

    lax.fori_loop(0, (n_tiles + 1) // 2, attn_body, 0)

    for n in range(N_KV_HEADS):
        acc = acc_ref[n]
        o = acc[0:HEAD_DIM, :] / acc[HEAD_DIM:HEAD_DIM + 1, :]
        for g in range(GROUP):
            c0 = (GROUP * n + g) * HEAD_DIM
            o_ref[:, c0:c0 + HEAD_DIM] = o[:, g * TQ:(g + 1) * TQ].T.astype(BF16)


def _attention(qT, iqT, iwT, kr, ikp, vT, batch, seq):
    T = kr.shape[0]
    nq = seq // TQ
    topk = min(TOPK_MAX, seq // 4)
    once = pl.Buffered(1)
    return pl.pallas_call(
        functools.partial(_attn_kernel, seq=seq, topk=topk),
        grid=(batch, nq),
        in_specs=[
            pl.BlockSpec((ATTN_WIDTH, TQ), lambda b, i: (0, b * nq + i)),
            pl.BlockSpec((IDX_WIDTH, TQ), lambda b, i: (0, b * nq + i)),
            pl.BlockSpec((LANES, TQ), lambda b, i: (0, b * nq + i)),
            pl.BlockSpec((seq, KV_WIDTH), lambda b, i: (b, 0), pipeline_mode=once),
            pl.BlockSpec((seq, LANES), lambda b, i: (b, 0), pipeline_mode=once),
            pl.BlockSpec((KV_WIDTH, seq), lambda b, i: (0, b), pipeline_mode=once),
        ],
        out_specs=pl.BlockSpec((TQ, ATTN_WIDTH), lambda b, i: (b * nq + i, 0)),
        out_shape=jax.ShapeDtypeStruct((T, ATTN_WIDTH), BF16),
        scratch_shapes=[
            pltpu.VMEM((seq, TQ), I32),
            pltpu.VMEM((seq, TQ), I16),
            pltpu.VMEM((seq, TQ), I16),
            pltpu.VMEM((N_KV_HEADS, V_ROWS, seq), BF16),
            pltpu.VMEM((1, TQ), I32),
            pltpu.VMEM((N_KV_HEADS, HEAD_DIM, GROUP * TQ), BF16),
            pltpu.VMEM((N_KV_HEADS, 1, GROUP * TQ), F32),
            pltpu.VMEM((N_KV_HEADS, V_ROWS, GROUP * TQ), F32),
            pltpu.VMEM((N_KV_HEADS, TK, GROUP * TQ), F32),
            pltpu.VMEM((N_KV_HEADS, TK, GROUP * TQ), F32),
            pltpu.VMEM((N_KV_HEADS, 1, GROUP * TQ), F32),
            pltpu.VMEM((N_KV_HEADS, 1, GROUP * TQ), F32),
        ],
        compiler_params=_params(("arbitrary", "arbitrary")),
        name="sparse_attn",
    )(qT, iqT, iwT, kr, ikp, vT)


def _outproj_kernel(a_ref, p_ref, wa_ref, wp_ref, x_ref, o_ref):
    o_ref[...] = (x_ref[...]
                  + jnp.dot(a_ref[...], wa_ref[...], preferred_element_type=F32)
                  + jnp.dot(p_ref[...], wp_ref[...], preferred_element_type=F32))


def _outproj(attn, pool, w_out, x2):
    T, D = x2.shape
    tm, tn = 512, 1024
    ka, kp = attn.shape[1], pool.shape[1]
    assert ka == kp
    return pl.pallas_call(
        _outproj_kernel,
        grid=(T // tm, D // tn),
        in_specs=[
            pl.BlockSpec((tm, ka), lambda i, j: (i, 0)),
            pl.BlockSpec((tm, kp), lambda i, j: (i, 0)),
            pl.BlockSpec((ka, tn), lambda i, j: (0, j)),
            pl.BlockSpec((kp, tn), lambda i, j: (1, j)),
            pl.BlockSpec((tm, tn), lambda i, j: (i, j)),
        ],
        out_specs=pl.BlockSpec((tm, tn), lambda i, j: (i, j)),
        out_shape=jax.ShapeDtypeStruct((T, D), F32),
        compiler_params=_params(("parallel", "parallel")),
        name="outproj",
    )(attn, pool, w_out, w_out, x2)


def _gateup_kernel(x_ref, g_ref, wg_ref, wu_ref, a_ref, h_ref):
    @pl.when(pl.program_id(1) == 0)
    def _():
        h_ref[...] = _rmsnorm_bf16(x_ref[...], g_ref[...])

    h = h_ref[...]
    gate = jnp.dot(h, wg_ref[...], preferred_element_type=F32)
    up = jnp.dot(h, wu_ref[...], preferred_element_type=F32)
    a_ref[...] = (gate * (1.0 / (1.0 + jnp.exp(-gate))) * up).astype(BF16)


def _gateup(x1, g, w_gate, w_up):
    T, D = x1.shape
    F = w_gate.shape[1]
    tm, tn = 512, 512
    return pl.pallas_call(
        _gateup_kernel,
        grid=(T // tm, F // tn),
        in_specs=[
            pl.BlockSpec((tm, D), lambda i, j: (i, 0)),
            pl.BlockSpec((1, D), lambda i, j: (0, 0)),
            pl.BlockSpec((D, tn), lambda i, j: (0, j)),
            pl.BlockSpec((D, tn), lambda i, j: (0, j)),
        ],
        out_specs=pl.BlockSpec((tm, tn), lambda i, j: (i, j)),
        out_shape=jax.ShapeDtypeStruct((T, F), BF16),
        scratch_shapes=[pltpu.VMEM((tm, D), BF16)],
        compiler_params=_params(("arbitrary", "arbitrary")),
        name="ffn_gateup",
    )(x1, g, w_gate, w_up)


def _down_kernel(a_ref, w_ref, x_ref, o_ref, acc_ref):
    k = pl.program_id(2)

    @pl.when(k == 0)
    def _():
        acc_ref[...] = x_ref[...]

    acc_ref[...] += jnp.dot(a_ref[...], w_ref[...], preferred_element_type=F32)

    @pl.when(k == pl.num_programs(2) - 1)
    def _():
        o_ref[...] = acc_ref[...]


def _down(a, w_down, x1):
    T, D = x1.shape
    F = a.shape[1]
    tm, tn = 512, 1024
    tk = F // 4 if (F // 4) % LANES == 0 else F
    return pl.pallas_call(
        _down_kernel,
        grid=(T // tm, D // tn, F // tk),
        in_specs=[
            pl.BlockSpec((tm, tk), lambda i, j, k: (i, k)),
            pl.BlockSpec((tk, tn), lambda i, j, k: (k, j)),
            pl.BlockSpec((tm, tn), lambda i, j, k: (i, j)),
        ],
        out_specs=pl.BlockSpec((tm, tn), lambda i, j, k: (i, j)),
        out_shape=jax.ShapeDtypeStruct((T, D), F32),
        scratch_shapes=[pltpu.VMEM((tm, tn), F32)],
        compiler_params=_params(("parallel", "parallel", "arbitrary")),
        name="ffn_down",
    )(a, w_down, x1)


def _final_norm_kernel(x_ref, g_ref, o_ref):
    x = x_ref[...]
    ms = jnp.mean(x * x, axis=-1, keepdims=True)
    o_ref[...] = x * lax.rsqrt(ms + EPS) * g_ref[...]


def _final_norm(x2, g):
    T, D = x2.shape
    tm = 256
    return pl.pallas_call(
        _final_norm_kernel,
        grid=(T // tm,),
        in_specs=[pl.BlockSpec((tm, D), lambda i: (i, 0)),
                  pl.BlockSpec((1, D), lambda i: (0, 0))],
        out_specs=pl.BlockSpec((tm, D), lambda i: (i, 0)),
        out_shape=jax.ShapeDtypeStruct((T, D), F32),
        compiler_params=_params(("parallel",)),
        name="final_norm",
    )(x2, g)


def _rope_table(seq, head_dim):
    rot = head_dim // ROT_FRACTION
    half = rot // 2
    inv_freq = ROPE_THETA ** (-jnp.arange(0, rot, 2, dtype=F32) / rot)
    ang = jnp.arange(seq).astype(F32)[:, None] * inv_freq[None, :]
    cos, sin = jnp.cos(ang), jnp.sin(ang)
    zeros = lambda n: jnp.zeros((seq, n), F32)
    c = jnp.concatenate([cos, cos, jnp.ones((seq, head_dim - rot), F32)], axis=1)
    sa = jnp.concatenate([-sin, zeros(head_dim - half)], axis=1)
    sb = jnp.concatenate([zeros(half), sin, zeros(head_dim - rot)], axis=1)
    reps = LANES // head_dim
    return jnp.stack([jnp.tile(t, (1, reps)) for t in (c, sa, sb)])


def kernel(x, norm_mix_g, w_in, w_pool, pool_scale, w_out, norm_ffn_g, w_gate, w_up, w_down,
           norm_final_g):
    B, S, D = x.shape
    T = B * S
    depth = w_in.shape[0]
    d_ff = w_gate.shape[2]
    pool_w = pool_scale.shape[1]
    f_pad = -(-d_ff // FFN_PAD) * FFN_PAD
    small0 = ATTN_WIDTH + 2 * KV_WIDTH + IDX_WIDTH
    small1 = small0 + IDX_DIM + IDX_HEADS

    rope_a = _rope_table(S, HEAD_DIM)
    rope_i = _rope_table(S, IDX_DIM)
    x2 = x.reshape(T, D)
    for l in range(depth):
        wl = w_in[l]
        w_main = jnp.concatenate([wl[:, :small0], wl[:, small1:]], axis=1).astype(BF16)
        w_small = jnp.concatenate(
            [wl[:, small0:small0 + IDX_DIM], jnp.zeros((D, LANES - IDX_DIM), F32),
             wl[:, small0 + IDX_DIM:small1], jnp.zeros((D, LANES - IDX_HEADS), F32)],
            axis=1).astype(BF16)
        pad_f = lambda w, axis: jnp.pad(
            w.astype(BF16), [(0, f_pad - d_ff) if a == axis else (0, 0) for a in range(2)])

        qT, kr, vT, iqT, ikp, iwT, u = _inproj(
            x2, norm_mix_g[l][None], w_main, w_small, rope_a, rope_i, S)
        pool = _pool(u, w_pool[l].astype(BF16), pool_scale[l][None], B, S)
        attn = _attention(qT, iqT, iwT, kr, ikp, vT, B, S)
        x2 = _outproj(attn, pool, w_out[l].astype(BF16), x2)
        a = _gateup(x2, norm_ffn_g[l][None], pad_f(w_gate[l], 1), pad_f(w_up[l], 1))
        x2 = _down(a, pad_f(w_down[l], 0), x2)
    return _final_norm(x2, norm_final_g[None]).reshape(B, S, D)
```

```python
import functools

import jax
import jax.numpy as jnp
import numpy as np
from jax import lax
from jax.experimental import pallas as pl
from jax.experimental.pallas import tpu as pltpu

N_HEADS = 16
N_KV_HEADS = 4
HEAD_DIM = 128
GROUP = N_HEADS // N_KV_HEADS
ATTN_WIDTH = N_HEADS * HEAD_DIM
KV_WIDTH = N_KV_HEADS * HEAD_DIM
IDX_HEADS = 16
IDX_DIM = 64
IDX_WIDTH = IDX_HEADS * IDX_DIM
TOPK_MAX = 256
POOL_WINDOWS = (2, 4, 8, 16)
MAX_WINDOW = max(POOL_WINDOWS)
ROPE_THETA = 500000.0
ROT_FRACTION = 4
EPS = 1e-6

LANES = 128
VMEM_LIMIT = 56 * 1024 * 1024

TQ = 128
TK = 512
V_ROWS = HEAD_DIM + 16
MASK_NEG = -1e30

I16_MIN = -2 ** 15
PACK = 16
LOG2_E = 1.4426950408889634
NEG_INF_KEY = int(np.array(-np.inf, np.float32).view(np.int32)) ^ 0x7FFFFFFF
MASK_NEG_BITS = int(np.array(MASK_NEG, np.float32).view(np.int32))

BF16 = jnp.bfloat16
F32 = jnp.float32
I32 = jnp.int32
I16 = jnp.int16


def _params(semantics):
    return pltpu.CompilerParams(dimension_semantics=semantics, vmem_limit_bytes=VMEM_LIMIT)


def _rope(z, tab_ref, half):
    return (z * tab_ref[0]
            + pltpu.roll(z, LANES - half, 1) * tab_ref[1]
            + pltpu.roll(z, half, 1) * tab_ref[2])


def _rope_rows(zt, tab_ref, head_dim):
    half = head_dim // ROT_FRACTION // 2
    c, s = tab_ref[0], tab_ref[1]
    parts = []
    for base in range(0, LANES, head_dim):
        x1 = zt[base:base + half]
        x2 = zt[base + half:base + 2 * half]
        parts += [x1 * c - x2 * s, x2 * c + x1 * s, zt[base + 2 * half:base + head_dim]]
    return jnp.concatenate(parts, axis=0)


def _rmsnorm_bf16(x, g):
    ms = jnp.mean(x * x, axis=-1, keepdims=True)
    return (x * lax.rsqrt(ms + EPS) * g).astype(BF16)


def _inproj_kernel(x_ref, g_ref, wm_ref, ws_ref, ra_ref, ri_ref, rta_ref, rti_ref,
                   qT_ref, kr_ref, vT_ref, iqT_ref, ikp_ref, iwT_ref, u_ref, h_ref):
    j = pl.program_id(1)
    q_scale = HEAD_DIM ** -0.5 * LOG2_E
    idx_scale = (IDX_DIM ** -0.5) * (IDX_HEADS ** -0.5)
    a_half = HEAD_DIM // ROT_FRACTION // 2
    i_half = IDX_DIM // ROT_FRACTION // 2

    @pl.when(j == 0)
    def _():
        h_ref[...] = _rmsnorm_bf16(x_ref[...], g_ref[...])
        zs = jnp.dot(h_ref[...], ws_ref[...], preferred_element_type=F32)
        ikp_ref[...] = _rope(zs[:, :LANES], ri_ref, i_half).astype(BF16)
        iwT_ref[...] = (zs[:, LANES:] * idx_scale).T

    z = jnp.dot(h_ref[...], wm_ref[...], preferred_element_type=F32)
    n_tiles = z.shape[1] // LANES

    @pl.when(j <= 1)
    def _():
        for t in range(n_tiles):
            zt = (z[:, t * LANES:(t + 1) * LANES] * q_scale).T
            qT_ref[t * LANES:(t + 1) * LANES, :] = _rope_rows(zt, rta_ref, HEAD_DIM).astype(BF16)

    @pl.when(j == 2)
    def _():
        for t in range(N_KV_HEADS):
            zz = _rope(z[:, t * LANES:(t + 1) * LANES], ra_ref, a_half)
            kr_ref[:, t * LANES:(t + 1) * LANES] = zz.astype(BF16)
        for t in range(N_KV_HEADS):
            c0 = KV_WIDTH + t * LANES
            vT_ref[t * LANES:(t + 1) * LANES, :] = z[:, c0:c0 + LANES].T.astype(BF16)

    @pl.when(j == 3)
    def _():
        for t in range(n_tiles):
            zt = z[:, t * LANES:(t + 1) * LANES].T
            iqT_ref[t * LANES:(t + 1) * LANES, :] = _rope_rows(zt, rti_ref, IDX_DIM).astype(BF16)

    @pl.when(j >= 4)
    def _():
        u_ref[...] = z


def _inproj(x2, g, w_main, w_small, rope_a, rope_i, rope_ta, rope_ti, seq):
    T, D = x2.shape
    tm = min(512, seq)
    tn = 1024
    nj = w_main.shape[1] // tn
    pool_w = w_main.shape[1] - 4 * tn
    sb = seq // tm
    out_shape = (
        jax.ShapeDtypeStruct((ATTN_WIDTH, T), BF16),
        jax.ShapeDtypeStruct((T, KV_WIDTH), BF16),
        jax.ShapeDtypeStruct((KV_WIDTH, T), BF16),
        jax.ShapeDtypeStruct((IDX_WIDTH, T), BF16),
        jax.ShapeDtypeStruct((T, LANES), BF16),
        jax.ShapeDtypeStruct((LANES, T), F32),
        jax.ShapeDtypeStruct((T, pool_w), F32),
    )
    return pl.pallas_call(
        _inproj_kernel,
        grid=(T // tm, nj),
        in_specs=[
            pl.BlockSpec((tm, D), lambda i, j: (i, 0), pipeline_mode=pl.Buffered(1)),
            pl.BlockSpec((1, D), lambda i, j: (0, 0)),
            pl.BlockSpec((D, tn), lambda i, j: (0, j)),
            pl.BlockSpec((D, 2 * LANES), lambda i, j: (0, 0), pipeline_mode=pl.Buffered(1)),
            pl.BlockSpec((3, tm, LANES), lambda i, j: (0, i % sb, 0)),
            pl.BlockSpec((3, tm, LANES), lambda i, j: (0, i % sb, 0)),
            pl.BlockSpec((2, rope_ta.shape[1], tm), lambda i, j: (0, 0, i % sb)),
            pl.BlockSpec((2, rope_ti.shape[1], tm), lambda i, j: (0, 0, i % sb)),
        ],
        out_specs=(
            pl.BlockSpec((tn, tm), lambda i, j: (jnp.minimum(j, 1), i)),
            pl.BlockSpec((tm, KV_WIDTH), lambda i, j: (i, 0)),
            pl.BlockSpec((KV_WIDTH, tm), lambda i, j: (0, i)),
            pl.BlockSpec((IDX_WIDTH, tm), lambda i, j: (0, i)),
            pl.BlockSpec((tm, LANES), lambda i, j: (i, 0)),
            pl.BlockSpec((LANES, tm), lambda i, j: (0, i)),
            pl.BlockSpec((tm, tn), lambda i, j: (i, jnp.clip(j - 4, 0, pool_w // tn - 1))),
        ),
        out_shape=out_shape,
        scratch_shapes=[pltpu.VMEM((tm, D), BF16)],
        compiler_params=_params(("arbitrary", "arbitrary")),
        name="inproj",
    )(x2, g, w_main, w_small, rope_a, rope_i, rope_ta, rope_ti)


def _pool_kernel(u_ref, wp_ref, ps_ref, o_ref, ext_ref):
    s = pl.program_id(1)
    ts = u_ref.shape[0]
    cg = wp_ref.shape[1]

    @pl.when(s == 0)
    def _():
        ext_ref[0:MAX_WINDOW, :] = jnp.zeros((MAX_WINDOW, ext_ref.shape[1]), F32)

    @pl.when(s > 0)
    def _():
        ext_ref[0:MAX_WINDOW, :] = ext_ref[ts:ts + MAX_WINDOW, :]

    ext_ref[MAX_WINDOW:MAX_WINDOW + ts, :] = u_ref[...]
    t = s * ts + lax.broadcasted_iota(I32, (ts, 1), 0)
    for g, w in enumerate(POOL_WINDOWS):
        c0 = g * cg
        cur = ext_ref[MAX_WINDOW:MAX_WINDOW + ts, c0:c0 + cg]
        acc = cur
        for back in range(1, w):
            acc = acc + ext_ref[MAX_WINDOW - back:MAX_WINDOW - back + ts, c0:c0 + cg]
        cnt = jnp.minimum(t + 1, w).astype(F32)
        diff = (acc / cnt - cur).astype(BF16)
        mixed = jnp.dot(diff, wp_ref[g], preferred_element_type=F32)
        o_ref[:, c0:c0 + cg] = (mixed * ps_ref[:, c0:c0 + cg]).astype(BF16)


def _pool(u, w_pool, pool_scale, batch, seq):
    T, C = u.shape
    ts = min(512, seq)
    ns = seq // ts
    G, cg, _ = w_pool.shape
    return pl.pallas_call(
        _pool_kernel,
        grid=(batch, ns),
        in_specs=[
            pl.BlockSpec((ts, C), lambda b, s: (b * ns + s, 0)),
            pl.BlockSpec((G, cg, cg), lambda b, s: (0, 0, 0)),
            pl.BlockSpec((1, C), lambda b, s: (0, 0)),
        ],
        out_specs=pl.BlockSpec((ts, C), lambda b, s: (b * ns + s, 0)),
        out_shape=jax.ShapeDtypeStruct((T, C), BF16),
        scratch_shapes=[pltpu.VMEM((ts + MAX_WINDOW, C), F32)],
        compiler_params=_params(("arbitrary", "arbitrary")),
        name="pool",
    )(u, w_pool, pool_scale)


def _attn_kernel(qT_ref, iqT_ref, iwT_ref, kr_ref, ikp_ref, vT_ref, o_ref,
                 key_ref, hi_ref, lo_ref, vte_ref, cut_ref, qg_ref, m_ref, acc_ref,
                 s0_ref, s1_ref, mt0_ref, mt1_ref, *, seq, topk):
    i = pl.program_id(1)
    n_tiles = (i * TQ + TQ + TK - 1) // TK
    row_bits = (seq - 1).bit_length()

    @pl.when(i == 0)
    def _():
        for n in range(N_KV_HEADS):
            vte_ref[n, 0:HEAD_DIM, :] = vT_ref[n * HEAD_DIM:(n + 1) * HEAD_DIM, :]
            vte_ref[n, HEAD_DIM:V_ROWS, :] = jnp.ones((V_ROWS - HEAD_DIM, seq), BF16)

    qpos = i * TQ + lax.broadcasted_iota(I32, (1, TQ), 1)

    def tile_rows(t):
        r0 = pl.multiple_of(t * TK, TK)
        return r0, r0 + lax.broadcasted_iota(I32, (TK, TQ), 0)

    iq = iqT_ref[...]
    idx_rhs = jnp.concatenate(
        [iq[h * IDX_DIM:(h + 1) * IDX_DIM, :] for h in range(IDX_HEADS)], axis=1)
    idx_rhs = jnp.concatenate([idx_rhs, jnp.zeros_like(idx_rhs)], axis=0)
    iw = iwT_ref[...]

    def score_body(t, carry):
        r0, rows = tile_rows(t)
        logits = jnp.dot(ikp_ref[pl.ds(r0, TK), :], idx_rhs, preferred_element_type=F32)
        sc = jnp.zeros((TK, TQ), F32)
        for h in range(IDX_HEADS):
            sc = sc + jnp.maximum(logits[:, h * TQ:(h + 1) * TQ], 0.0) * iw[h:h + 1, :]
        sc = jnp.where(rows <= qpos, sc + 0.0, -jnp.inf)
        bits = pltpu.bitcast(sc, I32)
        key = jnp.where(bits < 0, bits ^ 0x7FFFFFFF, bits)
        key_ref[pl.ds(r0, TK), :] = key
        hi_ref[pl.ds(r0, TK), :] = (key >> 16).astype(I16)
        lo_ref[pl.ds(r0, TK), :] = (((key ^ 0x8000) << 16) >> 16).astype(I16)
        return carry

    lax.fori_loop(0, n_tiles, score_body, 0)

    n_pairs = (n_tiles + 1) // 2

    @pl.when(n_tiles % 2 == 1)
    def _():
        pad = pl.ds(pl.multiple_of(n_tiles * TK, TK), TK)
        hi_ref[pad, :] = jnp.full((TK, TQ), NEG_INF_KEY >> 16, I16)
        lo_ref[pad, :] = jnp.full((TK, TQ), I16_MIN, I16)

    def rows16(v):
        return jnp.broadcast_to(v, (PACK, TQ)).astype(I16)

    def count16(ref, cand, op):
        c16 = rows16(cand)

        def body(t, accs):
            v = ref[pl.ds(pl.multiple_of(t * 2 * TK, 2 * TK), 2 * TK), :]
            accs = list(accs)
            for r in range(2 * TK // PACK):
                a = r % len(accs)
                hit = op(v[r * PACK:(r + 1) * PACK, :], c16)
                accs[a] = accs[a] + jnp.where(hit, jnp.int16(1), jnp.int16(0))
            return tuple(accs)

        zero = jnp.zeros((PACK, TQ), I16)
        accs = lax.fori_loop(0, n_pairs, body, (zero,) * 4)
        total = sum(a.astype(I32) for a in accs)
        return jnp.sum(total, axis=0, keepdims=True)

    ge = lambda v, c: v >= c
    gt = lambda v, c: v > c

    def half_search(ref, base, count_all):
        def bit(p, carry):
            lo, cnt_lo = carry
            cand = lo + jnp.left_shift(jnp.int32(1), 15 - p)
            cnt = base + count16(ref, cand, ge)
            ok = cnt >= topk
            return jnp.where(ok, cand, lo), jnp.where(ok, cnt, cnt_lo)
        return lax.fori_loop(0, 16, bit, (jnp.full((1, TQ), I16_MIN, I32), count_all))

    tau_hi, c_ge_hi = half_search(hi_ref, 0, jnp.broadcast_to(n_pairs * 2 * TK, (1, TQ)))
    c_gt_hi = count16(hi_ref, tau_hi, gt)
    tau_hi16 = rows16(tau_hi)

    def narrow_body(t, carry):
        sl = pl.ds(pl.multiple_of(t * 2 * TK, 2 * TK), 2 * TK)
        hi, lo = hi_ref[sl, :], lo_ref[sl, :]
        lo_ref[sl, :] = jnp.concatenate(
            [jnp.where(hi[r * PACK:(r + 1) * PACK, :] == tau_hi16,
                       lo[r * PACK:(r + 1) * PACK, :], jnp.int16(I16_MIN))
             for r in range(2 * TK // PACK)], axis=0)
        return carry

    lax.fori_loop(0, n_pairs, narrow_body, 0)
    tau_lo, c_ge = half_search(lo_ref, c_gt_hi, c_ge_hi)
    tau = tau_hi * 65536 + (tau_lo - I16_MIN)

    def count(pred):
        def body(t, c8):
            r0, rows = tile_rows(t)
            m = pred(key_ref[pl.ds(r0, TK), :], rows).astype(I32)
            return c8 + jnp.sum(m.reshape(TK // 8, 8, TQ), axis=0)
        c8 = lax.fori_loop(0, n_tiles, body, jnp.zeros((8, TQ), I32))
        return jnp.sum(c8, axis=0, keepdims=True)

    cut_ref[...] = jnp.full((1, TQ), seq, I32)
    tied = jnp.logical_and(c_ge > topk, tau > NEG_INF_KEY)

    @pl.when(jnp.max(tied.astype(I32)) > 0)
    def _():
        need = topk - count(lambda k, rows: k > tau)

        def row_bit(p, x):
            cand = x + jnp.left_shift(jnp.int32(1), row_bits - 1 - p)
            cnt = count(lambda k, rows: jnp.logical_and(k == tau, rows < cand))
            return jnp.where(cnt < need, cand, x)
        cut_ref[...] = lax.fori_loop(0, row_bits, row_bit, jnp.zeros((1, TQ), I32))

    cut = cut_ref[...]

    def mask_body(t, carry):
        r0, rows = tile_rows(t)
        k = key_ref[pl.ds(r0, TK), :]
        sel = jnp.logical_or(k > tau, jnp.logical_and(k == tau, rows <= cut))
        sel = jnp.logical_and(sel, rows <= qpos)
        key_ref[pl.ds(r0, TK), :] = jnp.where(sel, 0, MASK_NEG_BITS).astype(I32)
        return carry

    lax.fori_loop(0, n_tiles, mask_body, 0)

    for n in range(N_KV_HEADS):
        qg_ref[n] = jnp.concatenate(
            [qT_ref[(GROUP * n + g) * HEAD_DIM:(GROUP * n + g + 1) * HEAD_DIM, :]
             for g in range(GROUP)], axis=1)
        m_ref[n] = jnp.full((1, GROUP * TQ), MASK_NEG, F32)
        acc_ref[n] = jnp.zeros((V_ROWS, GROUP * TQ), F32)

    def score_stage(t, s_ref, mt_ref):
        r0 = pl.multiple_of(jnp.minimum(t, n_tiles - 1) * TK, TK)
        bias = pltpu.bitcast(key_ref[pl.ds(r0, TK), :], F32)
        bias = jnp.where(t < n_tiles, bias, MASK_NEG)
        bias = jnp.concatenate([bias] * GROUP, axis=1)
        for n in range(N_KV_HEADS):
            s = jnp.dot(kr_ref[pl.ds(r0, TK), n * HEAD_DIM:(n + 1) * HEAD_DIM], qg_ref[n],
                        preferred_element_type=F32) + bias
            s_ref[n] = s
            mt_ref[n] = jnp.max(s, axis=0, keepdims=True)

    def value_stage(t, s_ref, mt_ref):
        r0 = pl.multiple_of(jnp.minimum(t, n_tiles - 1) * TK, TK)
        for n in range(N_KV_HEADS):
            m = m_ref[n]
            m_new = jnp.maximum(m, mt_ref[n])
            p = jnp.exp2(s_ref[n] - m_new).astype(BF16)
            pv = jnp.dot(vte_ref[n, :, pl.ds(r0, TK)], p, preferred_element_type=F32)
            acc_ref[n] = acc_ref[n] * jnp.exp2(m - m_new) + pv
            m_ref[n] = m_new

    score_stage(0, s0_ref, mt0_ref)

    def attn_body(tt, carry):
        t = 2 * tt
        score_stage(t + 1, s1_ref, mt1_ref)
        value_stage(t, s0_ref, mt0_ref)
        score_stage(t + 2, s0_ref, mt0_ref)
        value_stage(t + 1, s1_ref, mt1_ref)
        return carry

    lax.fori_loop(0, (n_tiles + 1) // 2, attn_body, 0)

    for n in range(N_KV_HEADS):
        acc = acc_ref[n]
        o = acc[0:HEAD_DIM, :] / acc[HEAD_DIM:HEAD_DIM + 1, :]
        for g in range(GROUP):
            c0 = (GROUP * n + g) * HEAD_DIM
            o_ref[:, c0:c0 + HEAD_DIM] = o[:, g * TQ:(g + 1) * TQ].T.astype(BF16)


def _attention(qT, iqT, iwT, kr, ikp, vT, batch, seq):
    T = kr.shape[0]
    nq = seq // TQ
    topk = min(TOPK_MAX, seq // 4)
    assert seq % (2 * TK) == 0 and seq % TQ == 0
    once = pl.Buffered(1)
    return pl.pallas_call(
        functools.partial(_attn_kernel, seq=seq, topk=topk),
        grid=(batch, nq),
        in_specs=[
            pl.BlockSpec((ATTN_WIDTH, TQ), lambda b, i: (0, b * nq + i)),
            pl.BlockSpec((IDX_WIDTH, TQ), lambda b, i: (0, b * nq + i)),
            pl.BlockSpec((LANES, TQ), lambda b, i: (0, b * nq + i)),
            pl.BlockSpec((seq, KV_WIDTH), lambda b, i: (b, 0), pipeline_mode=once),
            pl.BlockSpec((seq, LANES), lambda b, i: (b, 0), pipeline_mode=once),
            pl.BlockSpec((KV_WIDTH, seq), lambda b, i: (0, b), pipeline_mode=once),
        ],
        out_specs=pl.BlockSpec((TQ, ATTN_WIDTH), lambda b, i: (b * nq + i, 0)),
        out_shape=jax.ShapeDtypeStruct((T, ATTN_WIDTH), BF16),
        scratch_shapes=[
            pltpu.VMEM((seq, TQ), I32),
            pltpu.VMEM((seq, TQ), I16),
            pltpu.VMEM((seq, TQ), I16),
            pltpu.VMEM((N_KV_HEADS, V_ROWS, seq), BF16),
            pltpu.VMEM((1, TQ), I32),
            pltpu.VMEM((N_KV_HEADS, HEAD_DIM, GROUP * TQ), BF16),
            pltpu.VMEM((N_KV_HEADS, 1, GROUP * TQ), F32),
            pltpu.VMEM((N_KV_HEADS, V_ROWS, GROUP * TQ), F32),
            pltpu.VMEM((N_KV_HEADS, TK, GROUP * TQ), F32),
            pltpu.VMEM((N_KV_HEADS, TK, GROUP * TQ), F32),
            pltpu.VMEM((N_KV_HEADS, 1, GROUP * TQ), F32),
            pltpu.VMEM((N_KV_HEADS, 1, GROUP * TQ), F32),
        ],
        compiler_params=_params(("arbitrary", "arbitrary")),
        name="sparse_attn",
    )(qT, iqT, iwT, kr, ikp, vT)


def _outproj_kernel(a_ref, p_ref, wa_ref, wp_ref, x_ref, o_ref):
    o_ref[...] = (x_ref[...]
                  + jnp.dot(a_ref[...], wa_ref[...], preferred_element_type=F32)
                  + jnp.dot(p_ref[...], wp_ref[...], preferred_element_type=F32))


def _outproj(attn, pool, w_out, x2):
    T, D = x2.shape
    tm, tn = 512, 1024
    ka, kp = attn.shape[1], pool.shape[1]
    assert ka == kp
    return pl.pallas_call(
        _outproj_kernel,
        grid=(T // tm, D // tn),
        in_specs=[
            pl.BlockSpec((tm, ka), lambda i, j: (i, 0)),
            pl.BlockSpec((tm, kp), lambda i, j: (i, 0)),
            pl.BlockSpec((ka, tn), lambda i, j: (0, j)),
            pl.BlockSpec((kp, tn), lambda i, j: (1, j)),
            pl.BlockSpec((tm, tn), lambda i, j: (i, j)),
        ],
        out_specs=pl.BlockSpec((tm, tn), lambda i, j: (i, j)),
        out_shape=jax.ShapeDtypeStruct((T, D), F32),
        compiler_params=_params(("parallel", "parallel")),
        name="outproj",
    )(attn, pool, w_out, w_out, x2)


def _gateup_kernel(x_ref, g_ref, wg_ref, wu_ref, a_ref, h_ref):
    @pl.when(pl.program_id(1) == 0)
    def _():
        h_ref[...] = _rmsnorm_bf16(x_ref[...], g_ref[...])

    h = h_ref[...]
    gate = jnp.dot(h, wg_ref[...], preferred_element_type=F32)
    up = jnp.dot(h, wu_ref[...], preferred_element_type=F32)
    a_ref[...] = (gate * (1.0 / (1.0 + jnp.exp(-gate))) * up).astype(BF16)


def _gateup(x1, g, w_gate, w_up):
    T, D = x1.shape
    F = w_gate.shape[1]
    tm, tn = 512, 512
    return pl.pallas_call(
        _gateup_kernel,
        grid=(T // tm, pl.cdiv(F, tn)),
        in_specs=[
            pl.BlockSpec((tm, D), lambda i, j: (i, 0)),
            pl.BlockSpec((1, D), lambda i, j: (0, 0)),
            pl.BlockSpec((D, tn), lambda i, j: (0, j)),
            pl.BlockSpec((D, tn), lambda i, j: (0, j)),
        ],
        out_specs=pl.BlockSpec((tm, tn), lambda i, j: (i, j)),
        out_shape=jax.ShapeDtypeStruct((T, F), BF16),
        scratch_shapes=[pltpu.VMEM((tm, D), BF16)],
        compiler_params=_params(("arbitrary", "arbitrary")),
        name="ffn_gateup",
    )(x1, g, w_gate, w_up)


def _down_kernel(a_ref, w_ref, x_ref, o_ref, acc_ref):
    k = pl.program_id(2)

    @pl.when(k == 0)
    def _():
        acc_ref[...] = x_ref[...]

    acc_ref[...] += jnp.dot(a_ref[...], w_ref[...], preferred_element_type=F32)

    @pl.when(k == pl.num_programs(2) - 1)
    def _():
        o_ref[...] = acc_ref[...]


def _down(a, w_down, x1):
    T, D = x1.shape
    F = a.shape[1]
    tm, tn = 512, 1024
    tk = F // 2 if (F // 2) % LANES == 0 else F
    return pl.pallas_call(
        _down_kernel,
        grid=(T // tm, D // tn, F // tk),
        in_specs=[
            pl.BlockSpec((tm, tk), lambda i, j, k: (i, k)),
            pl.BlockSpec((tk, tn), lambda i, j, k: (k, j)),
            pl.BlockSpec((tm, tn), lambda i, j, k: (i, j)),
        ],
        out_specs=pl.BlockSpec((tm, tn), lambda i, j, k: (i, j)),
        out_shape=jax.ShapeDtypeStruct((T, D), F32),
        scratch_shapes=[pltpu.VMEM((tm, tn), F32)],
        compiler_params=_params(("parallel", "parallel", "arbitrary")),
        name="ffn_down",
    )(a, w_down, x1)


def _final_norm_kernel(x_ref, g_ref, o_ref):
    x = x_ref[...]
    ms = jnp.mean(x * x, axis=-1, keepdims=True)
    o_ref[...] = x * lax.rsqrt(ms + EPS) * g_ref[...]


def _final_norm(x2, g):
    T, D = x2.shape
    tm = 256
    return pl.pallas_call(
        _final_norm_kernel,
        grid=(T // tm,),
        in_specs=[pl.BlockSpec((tm, D), lambda i: (i, 0)),
                  pl.BlockSpec((1, D), lambda i: (0, 0))],
        out_specs=pl.BlockSpec((tm, D), lambda i: (i, 0)),
        out_shape=jax.ShapeDtypeStruct((T, D), F32),
        compiler_params=_params(("parallel",)),
        name="final_norm",
    )(x2, g)


def _rope_table(seq, head_dim):
    rot = head_dim // ROT_FRACTION
    half = rot // 2
    inv_freq = ROPE_THETA ** (-jnp.arange(0, rot, 2, dtype=F32) / rot)
    ang = jnp.arange(seq).astype(F32)[:, None] * inv_freq[None, :]
    cos, sin = jnp.cos(ang), jnp.sin(ang)
    feature_major = jnp.stack([cos.T, sin.T])
    zeros = lambda n: jnp.zeros((seq, n), F32)
    c = jnp.concatenate([cos, cos, jnp.ones((seq, head_dim - rot), F32)], axis=1)
    sa = jnp.concatenate([-sin, zeros(head_dim - half)], axis=1)
    sb = jnp.concatenate([zeros(half), sin, zeros(head_dim - rot)], axis=1)
    reps = LANES // head_dim
    return jnp.stack([jnp.tile(t, (1, reps)) for t in (c, sa, sb)]), feature_major


def kernel(x, norm_mix_g, w_in, w_pool, pool_scale, w_out, norm_ffn_g, w_gate, w_up, w_down,
           norm_final_g):
    B, S, D = x.shape
    T = B * S
    depth = w_in.shape[0]
    pool_w = pool_scale.shape[1]
    small0 = ATTN_WIDTH + 2 * KV_WIDTH + IDX_WIDTH
    small1 = small0 + IDX_DIM + IDX_HEADS

    rope_a, rope_ta = _rope_table(S, HEAD_DIM)
    rope_i, rope_ti = _rope_table(S, IDX_DIM)
    x2 = x.reshape(T, D)
    for l in range(depth):
        wl = w_in[l].astype(BF16)
        w_main = jnp.concatenate([wl[:, :small0], wl[:, small1:]], axis=1)
        w_small = jnp.concatenate(
            [wl[:, small0:small0 + IDX_DIM], jnp.zeros((D, LANES - IDX_DIM), BF16),
             wl[:, small0 + IDX_DIM:small1], jnp.zeros((D, LANES - IDX_HEADS), BF16)],
            axis=1)

        qT, kr, vT, iqT, ikp, iwT, u = _inproj(
            x2, norm_mix_g[l][None], w_main, w_small, rope_a, rope_i, rope_ta, rope_ti, S)
        pool = _pool(u, w_pool[l].astype(BF16), pool_scale[l][None], B, S)
        attn = _attention(qT, iqT, iwT, kr, ikp, vT, B, S)
        x2 = _outproj(attn, pool, w_out[l].astype(BF16), x2)
        a = _gateup(x2, norm_ffn_g[l][None], w_gate[l].astype(BF16), w_up[l].astype(BF16))
        x2 = _down(a, w_down[l].astype(BF16), x2)
    return _final_norm(x2, norm_final_g[None]).reshape(B, S, D)
```

```python
import functools

import jax
import jax.numpy as jnp
import numpy as np
from jax import lax
from jax.experimental import pallas as pl
from jax.experimental.pallas import tpu as pltpu

N_HEADS = 16
N_KV_HEADS = 4
HEAD_DIM = 128
GROUP = N_HEADS // N_KV_HEADS
ATTN_WIDTH = N_HEADS * HEAD_DIM
KV_WIDTH = N_KV_HEADS * HEAD_DIM
IDX_HEADS = 16
IDX_DIM = 64
IDX_WIDTH = IDX_HEADS * IDX_DIM
TOPK_MAX = 256
POOL_WINDOWS = (2, 4, 8, 16)
MAX_WINDOW = max(POOL_WINDOWS)
ROPE_THETA = 500000.0
ROT_FRACTION = 4
EPS = 1e-6

LANES = 128
VMEM_LIMIT = 56 * 1024 * 1024

TQ = 128
TK = 512
V_ROWS = HEAD_DIM + 16
MASK_NEG = -1e30

I16_MIN = -2 ** 15
PACK = 16
LOG2_E = 1.4426950408889634
NEG_INF_KEY = int(np.array(-np.inf, np.float32).view(np.int32)) ^ 0x7FFFFFFF
MASK_NEG_BITS = int(np.array(MASK_NEG, np.float32).view(np.int32))

BF16 = jnp.bfloat16
F32 = jnp.float32
I32 = jnp.int32
I16 = jnp.int16


def _params(semantics):
    return pltpu.CompilerParams(dimension_semantics=semantics, vmem_limit_bytes=VMEM_LIMIT)


def _rope(z, tab_ref, half):
    return (z * tab_ref[0]
            + pltpu.roll(z, LANES - half, 1) * tab_ref[1]
            + pltpu.roll(z, half, 1) * tab_ref[2])


def _rope_rows(zt, tab_ref, head_dim):
    half = head_dim // ROT_FRACTION // 2
    c, s = tab_ref[0], tab_ref[1]
    parts = []
    for base in range(0, LANES, head_dim):
        x1 = zt[base:base + half]
        x2 = zt[base + half:base + 2 * half]
        parts += [x1 * c - x2 * s, x2 * c + x1 * s, zt[base + 2 * half:base + head_dim]]
    return jnp.concatenate(parts, axis=0)


def _rmsnorm_bf16(x, g):
    ms = jnp.mean(x * x, axis=-1, keepdims=True)
    return (x * lax.rsqrt(ms + EPS) * g).astype(BF16)


def _inproj_kernel(x_ref, g_ref, wm_ref, ws_ref, ra_ref, ri_ref, rta_ref, rti_ref,
                   qT_ref, kr_ref, vT_ref, iqT_ref, ikp_ref, iwT_ref, u_ref, h_ref):
    j = pl.program_id(1)
    q_scale = HEAD_DIM ** -0.5 * LOG2_E
    idx_scale = (IDX_DIM ** -0.5) * (IDX_HEADS ** -0.5)
    a_half = HEAD_DIM // ROT_FRACTION // 2
    i_half = IDX_DIM // ROT_FRACTION // 2

    @pl.when(j == 0)
    def _():
        h_ref[...] = _rmsnorm_bf16(x_ref[...], g_ref[...])
        zs = jnp.dot(h_ref[...], ws_ref[...], preferred_element_type=F32)
        ikp_ref[...] = _rope(zs[:, :LANES], ri_ref, i_half).astype(BF16)
        iwT_ref[...] = (zs[:, LANES:] * idx_scale).T

    z = jnp.dot(h_ref[...], wm_ref[...], preferred_element_type=F32)
    n_tiles = z.shape[1] // LANES

    @pl.when(j <= 1)
    def _():
        for t in range(n_tiles):
            zt = (z[:, t * LANES:(t + 1) * LANES] * q_scale).T
            qT_ref[t * LANES:(t + 1) * LANES, :] = _rope_rows(zt, rta_ref, HEAD_DIM).astype(BF16)

    @pl.when(j == 2)
    def _():
        for t in range(N_KV_HEADS):
            zz = _rope(z[:, t * LANES:(t + 1) * LANES], ra_ref, a_half)
            kr_ref[:, t * LANES:(t + 1) * LANES] = zz.astype(BF16)
        for t in range(N_KV_HEADS):
            c0 = KV_WIDTH + t * LANES
            vT_ref[t * LANES:(t + 1) * LANES, :] = z[:, c0:c0 + LANES].T.astype(BF16)

    @pl.when(j == 3)
    def _():
        for t in range(n_tiles):
            zt = z[:, t * LANES:(t + 1) * LANES].T
            iqT_ref[t * LANES:(t + 1) * LANES, :] = _rope_rows(zt, rti_ref, IDX_DIM).astype(BF16)

    @pl.when(j >= 4)
    def _():
        u_ref[...] = z


def _inproj(x2, g, w_main, w_small, rope_a, rope_i, rope_ta, rope_ti, seq):
    T, D = x2.shape
    tm = min(512, seq)
    tn = 1024
    nj = w_main.shape[1] // tn
    pool_w = w_main.shape[1] - 4 * tn
    sb = seq // tm
    out_shape = (
        jax.ShapeDtypeStruct((ATTN_WIDTH, T), BF16),
        jax.ShapeDtypeStruct((T, KV_WIDTH), BF16),
        jax.ShapeDtypeStruct((KV_WIDTH, T), BF16),
        jax.ShapeDtypeStruct((IDX_WIDTH, T), BF16),
        jax.ShapeDtypeStruct((T, LANES), BF16),
        jax.ShapeDtypeStruct((LANES, T), F32),
        jax.ShapeDtypeStruct((T, pool_w), F32),
    )
    return pl.pallas_call(
        _inproj_kernel,
        grid=(T // tm, nj),
        in_specs=[
            pl.BlockSpec((tm, D), lambda i, j: (i, 0), pipeline_mode=pl.Buffered(1)),
            pl.BlockSpec((1, D), lambda i, j: (0, 0)),
            pl.BlockSpec((D, tn), lambda i, j: (0, j)),
            pl.BlockSpec((D, 2 * LANES), lambda i, j: (0, 0), pipeline_mode=pl.Buffered(1)),
            pl.BlockSpec((3, tm, LANES), lambda i, j: (0, i % sb, 0)),
            pl.BlockSpec((3, tm, LANES), lambda i, j: (0, i % sb, 0)),
            pl.BlockSpec((2, rope_ta.shape[1], tm), lambda i, j: (0, 0, i % sb)),
            pl.BlockSpec((2, rope_ti.shape[1], tm), lambda i, j: (0, 0, i % sb)),
        ],
        out_specs=(
            pl.BlockSpec((tn, tm), lambda i, j: (jnp.minimum(j, 1), i)),
            pl.BlockSpec((tm, KV_WIDTH), lambda i, j: (i, 0)),
            pl.BlockSpec((KV_WIDTH, tm), lambda i, j: (0, i)),
            pl.BlockSpec((IDX_WIDTH, tm), lambda i, j: (0, i)),
            pl.BlockSpec((tm, LANES), lambda i, j: (i, 0)),
            pl.BlockSpec((LANES, tm), lambda i, j: (0, i)),
            pl.BlockSpec((tm, tn), lambda i, j: (i, jnp.clip(j - 4, 0, pool_w // tn - 1))),
        ),
        out_shape=out_shape,
        scratch_shapes=[pltpu.VMEM((tm, D), BF16)],
        compiler_params=_params(("arbitrary", "arbitrary")),
        name="inproj",
    )(x2, g, w_main, w_small, rope_a, rope_i, rope_ta, rope_ti)


def _pool_kernel(u_ref, wp_ref, ps_ref, o_ref, ext_ref):
    s = pl.program_id(1)
    ts = u_ref.shape[0]
    cg = wp_ref.shape[1]

    @pl.when(s == 0)
    def _():
        ext_ref[0:MAX_WINDOW, :] = jnp.zeros((MAX_WINDOW, ext_ref.shape[1]), F32)

    @pl.when(s > 0)
    def _():
        ext_ref[0:MAX_WINDOW, :] = ext_ref[ts:ts + MAX_WINDOW, :]

    ext_ref[MAX_WINDOW:MAX_WINDOW + ts, :] = u_ref[...]
    t = s * ts + lax.broadcasted_iota(I32, (ts, 1), 0)
    for g, w in enumerate(POOL_WINDOWS):
        c0 = g * cg
        cur = ext_ref[MAX_WINDOW:MAX_WINDOW + ts, c0:c0 + cg]
        acc = cur
        for back in range(1, w):
            acc = acc + ext_ref[MAX_WINDOW - back:MAX_WINDOW - back + ts, c0:c0 + cg]
        cnt = jnp.minimum(t + 1, w).astype(F32)
        diff = (acc / cnt - cur).astype(BF16)
        mixed = jnp.dot(diff, wp_ref[g], preferred_element_type=F32)
        o_ref[:, c0:c0 + cg] = (mixed * ps_ref[:, c0:c0 + cg]).astype(BF16)


def _pool(u, w_pool, pool_scale, batch, seq):
    T, C = u.shape
    ts = min(512, seq)
    ns = seq // ts
    G, cg, _ = w_pool.shape
    return pl.pallas_call(
        _pool_kernel,
        grid=(batch, ns),
        in_specs=[
            pl.BlockSpec((ts, C), lambda b, s: (b * ns + s, 0)),
            pl.BlockSpec((G, cg, cg), lambda b, s: (0, 0, 0)),
            pl.BlockSpec((1, C), lambda b, s: (0, 0)),
        ],
        out_specs=pl.BlockSpec((ts, C), lambda b, s: (b * ns + s, 0)),
        out_shape=jax.ShapeDtypeStruct((T, C), BF16),
        scratch_shapes=[pltpu.VMEM((ts + MAX_WINDOW, C), F32)],
        compiler_params=_params(("arbitrary", "arbitrary")),
        name="pool",
    )(u, w_pool, pool_scale)


def _attn_kernel(qT_ref, iqT_ref, iwT_ref, kr_ref, ikp_ref, vT_ref, o_ref,
                 key_ref, hi_ref, lo_ref, vte_ref, cut_ref, qg_ref, m_ref, acc_ref,
                 s0_ref, s1_ref, mt0_ref, mt1_ref, *, seq, topk):
    i = pl.program_id(1)
    n_tiles = (i * TQ + TQ + TK - 1) // TK
    row_bits = (seq - 1).bit_length()

    @pl.when(i == 0)
    def _():
        for n in range(N_KV_HEADS):
            vte_ref[n, 0:HEAD_DIM, :] = vT_ref[n * HEAD_DIM:(n + 1) * HEAD_DIM, :]
            vte_ref[n, HEAD_DIM:V_ROWS, :] = jnp.ones((V_ROWS - HEAD_DIM, seq), BF16)

    qpos = i * TQ + lax.broadcasted_iota(I32, (1, TQ), 1)

    def tile_rows(t):
        r0 = pl.multiple_of(t * TK, TK)
        return r0, r0 + lax.broadcasted_iota(I32, (TK, TQ), 0)

    iq = iqT_ref[...]
    idx_rhs = jnp.concatenate(
        [iq[h * IDX_DIM:(h + 1) * IDX_DIM, :] for h in range(IDX_HEADS)], axis=1)
    idx_rhs = jnp.concatenate([idx_rhs, jnp.zeros_like(idx_rhs)], axis=0)
    iw = iwT_ref[...]

    def score_body(t, carry):
        r0, rows = tile_rows(t)
        logits = jnp.dot(ikp_ref[pl.ds(r0, TK), :], idx_rhs, preferred_element_type=F32)
        sc = jnp.zeros((TK, TQ), F32)
        for h in range(IDX_HEADS):
            sc = sc + jnp.maximum(logits[:, h * TQ:(h + 1) * TQ], 0.0) * iw[h:h + 1, :]
        sc = jnp.where(rows <= qpos, sc + 0.0, -jnp.inf)
        bits = pltpu.bitcast(sc, I32)
        key = jnp.where(bits < 0, bits ^ 0x7FFFFFFF, bits)
        key_ref[pl.ds(r0, TK), :] = key
        hi_ref[pl.ds(r0, TK), :] = (key >> 16).astype(I16)
        lo_ref[pl.ds(r0, TK), :] = (((key ^ 0x8000) << 16) >> 16).astype(I16)
        return carry

    lax.fori_loop(0, n_tiles, score_body, 0)

    n_pairs = (n_tiles + 1) // 2

    @pl.when(n_tiles % 2 == 1)
    def _():
        pad = pl.ds(pl.multiple_of(n_tiles * TK, TK), TK)
        hi_ref[pad, :] = jnp.full((TK, TQ), NEG_INF_KEY >> 16, I16)
        lo_ref[pad, :] = jnp.full((TK, TQ), I16_MIN, I16)

    def rows16(v):
        return jnp.broadcast_to(v, (PACK, TQ)).astype(I16)

    def count16(ref, cand, op):
        c16 = rows16(cand)

        def body(t, accs):
            v = ref[pl.ds(pl.multiple_of(t * 2 * TK, 2 * TK), 2 * TK), :]
            accs = list(accs)
            for r in range(2 * TK // PACK):
                a = r % len(accs)
                hit = op(v[r * PACK:(r + 1) * PACK, :], c16)
                accs[a] = accs[a] + jnp.where(hit, jnp.int16(1), jnp.int16(0))
            return tuple(accs)

        zero = jnp.zeros((PACK, TQ), I16)
        accs = lax.fori_loop(0, n_pairs, body, (zero,) * 4)
        total = sum(a.astype(I32) for a in accs)
        return jnp.sum(total, axis=0, keepdims=True)

    ge = lambda v, c: v >= c
    gt = lambda v, c: v > c

    def half_search(ref, base, count_all):
        def bit(p, carry):
            lo, cnt_lo = carry
            cand = lo + jnp.left_shift(jnp.int32(1), 15 - p)
            cnt = base + count16(ref, cand, ge)
            ok = cnt >= topk
            return jnp.where(ok, cand, lo), jnp.where(ok, cnt, cnt_lo)
        return lax.fori_loop(0, 16, bit, (jnp.full((1, TQ), I16_MIN, I32), count_all))

    tau_hi, c_ge_hi = half_search(hi_ref, 0, jnp.broadcast_to(n_pairs * 2 * TK, (1, TQ)))
    c_gt_hi = count16(hi_ref, tau_hi, gt)
    tau_hi16 = rows16(tau_hi)

    def narrow_body(t, carry):
        sl = pl.ds(pl.multiple_of(t * 2 * TK, 2 * TK), 2 * TK)
        hi, lo = hi_ref[sl, :], lo_ref[sl, :]
        lo_ref[sl, :] = jnp.concatenate(
            [jnp.where(hi[r * PACK:(r + 1) * PACK, :] == tau_hi16,
                       lo[r * PACK:(r + 1) * PACK, :], jnp.int16(I16_MIN))
             for r in range(2 * TK // PACK)], axis=0)
        return carry

    lax.fori_loop(0, n_pairs, narrow_body, 0)
    tau_lo, c_ge = half_search(lo_ref, c_gt_hi, c_ge_hi)
    tau = tau_hi * 65536 + (tau_lo - I16_MIN)

    def count(pred):
        def body(t, c8):
            r0, rows = tile_rows(t)
            m = pred(key_ref[pl.ds(r0, TK), :], rows).astype(I32)
            return c8 + jnp.sum(m.reshape(TK // 8, 8, TQ), axis=0)
        c8 = lax.fori_loop(0, n_tiles, body, jnp.zeros((8, TQ), I32))
        return jnp.sum(c8, axis=0, keepdims=True)

    cut_ref[...] = jnp.full((1, TQ), seq, I32)
    tied = jnp.logical_and(c_ge > topk, tau > NEG_INF_KEY)

    @pl.when(jnp.max(tied.astype(I32)) > 0)
    def _():
        need = topk - count(lambda k, rows: k > tau)

        def row_bit(p, x):
            cand = x + jnp.left_shift(jnp.int32(1), row_bits - 1 - p)
            cnt = count(lambda k, rows: jnp.logical_and(k == tau, rows < cand))
            return jnp.where(cnt < need, cand, x)
        cut_ref[...] = lax.fori_loop(0, row_bits, row_bit, jnp.zeros((1, TQ), I32))

    cut = cut_ref[...]

    def mask_body(t, carry):
        r0, rows = tile_rows(t)
        k = key_ref[pl.ds(r0, TK), :]
        sel = jnp.logical_or(k > tau, jnp.logical_and(k == tau, rows <= cut))
        sel = jnp.logical_and(sel, rows <= qpos)
        key_ref[pl.ds(r0, TK), :] = jnp.where(sel, 0, MASK_NEG_BITS).astype(I32)
        return carry

    lax.fori_loop(0, n_tiles, mask_body, 0)

    for n in range(N_KV_HEADS):
        qg_ref[n] = jnp.concatenate(
            [qT_ref[(GROUP * n + g) * HEAD_DIM:(GROUP * n + g + 1) * HEAD_DIM, :]
             for g in range(GROUP)], axis=1)
        m_ref[n] = jnp.full((1, GROUP * TQ), MASK_NEG, F32)
        acc_ref[n] = jnp.zeros((V_ROWS, GROUP * TQ), F32)

    def score_stage(t, s_ref, mt_ref):
        r0 = pl.multiple_of(jnp.minimum(t, n_tiles - 1) * TK, TK)
        bias = pltpu.bitcast(key_ref[pl.ds(r0, TK), :], F32)
        bias = jnp.where(t < n_tiles, bias, MASK_NEG)
        bias = jnp.concatenate([bias] * GROUP, axis=1)
        for n in range(N_KV_HEADS):
            s = jnp.dot(kr_ref[pl.ds(r0, TK), n * HEAD_DIM:(n + 1) * HEAD_DIM], qg_ref[n],
                        preferred_element_type=F32) + bias
            s_ref[n] = s
            mt_ref[n] = jnp.max(s, axis=0, keepdims=True)

    def value_stage(t, s_ref, mt_ref):
        r0 = pl.multiple_of(jnp.minimum(t, n_tiles - 1) * TK, TK)
        for n in range(N_KV_HEADS):
            m = m_ref[n]
            m_new = jnp.maximum(m, mt_ref[n])
            p = jnp.exp2((s_ref[n] - m_new).astype(BF16))
            pv = jnp.dot(vte_ref[n, :, pl.ds(r0, TK)], p, preferred_element_type=F32)
            acc_ref[n] = acc_ref[n] * jnp.exp2(m - m_new) + pv
            m_ref[n] = m_new

    score_stage(0, s0_ref, mt0_ref)

    def attn_body(tt, carry):
        t = 2 * tt
        score_stage(t + 1, s1_ref, mt1_ref)
        value_stage(t, s0_ref, mt0_ref)
        score_stage(t + 2, s0_ref, mt0_ref)
        value_stage(t + 1, s1_ref, mt1_ref)
        return carry

    lax.fori_loop(0, (n_tiles + 1) // 2, attn_body, 0)

    for n in range(N_KV_HEADS):
        acc = acc_ref[n]
        o = acc[0:HEAD_DIM, :] / acc[HEAD_DIM:HEAD_DIM + 1, :]
        for g in range(GROUP):
            c0 = (GROUP * n + g) * HEAD_DIM
            o_ref[:, c0:c0 + HEAD_DIM] = o[:, g * TQ:(g + 1) * TQ].T.astype(BF16)


def _attention(qT, iqT, iwT, kr, ikp, vT, batch, seq):
    T = kr.shape[0]
    nq = seq // TQ
    topk = min(TOPK_MAX, seq // 4)
    assert seq % (2 * TK) == 0 and seq % TQ == 0
    once = pl.Buffered(1)
    return pl.pallas_call(
        functools.partial(_attn_kernel, seq=seq, topk=topk),
        grid=(batch, nq),
        in_specs=[
            pl.BlockSpec((ATTN_WIDTH, TQ), lambda b, i: (0, b * nq + i)),
            pl.BlockSpec((IDX_WIDTH, TQ), lambda b, i: (0, b * nq + i)),
            pl.BlockSpec((LANES, TQ), lambda b, i: (0, b * nq + i)),
            pl.BlockSpec((seq, KV_WIDTH), lambda b, i: (b, 0), pipeline_mode=once),
            pl.BlockSpec((seq, LANES), lambda b, i: (b, 0), pipeline_mode=once),
            pl.BlockSpec((KV_WIDTH, seq), lambda b, i: (0, b), pipeline_mode=once),
        ],
        out_specs=pl.BlockSpec((TQ, ATTN_WIDTH), lambda b, i: (b * nq + i, 0)),
        out_shape=jax.ShapeDtypeStruct((T, ATTN_WIDTH), BF16),
        scratch_shapes=[
            pltpu.VMEM((seq, TQ), I32),
            pltpu.VMEM((seq, TQ), I16),
            pltpu.VMEM((seq, TQ), I16),
            pltpu.VMEM((N_KV_HEADS, V_ROWS, seq), BF16),
            pltpu.VMEM((1, TQ), I32),
            pltpu.VMEM((N_KV_HEADS, HEAD_DIM, GROUP * TQ), BF16),
            pltpu.VMEM((N_KV_HEADS, 1, GROUP * TQ), F32),
            pltpu.VMEM((N_KV_HEADS, V_ROWS, GROUP * TQ), F32),
            pltpu.VMEM((N_KV_HEADS, TK, GROUP * TQ), F32),
            pltpu.VMEM((N_KV_HEADS, TK, GROUP * TQ), F32),
            pltpu.VMEM((N_KV_HEADS, 1, GROUP * TQ), F32),
            pltpu.VMEM((N_KV_HEADS, 1, GROUP * TQ), F32),
        ],
        compiler_params=_params(("arbitrary", "arbitrary")),
        name="sparse_attn",
    )(qT, iqT, iwT, kr, ikp, vT)


def _outproj_kernel(a_ref, p_ref, wa_ref, wp_ref, x_ref, o_ref):
    o_ref[...] = (x_ref[...]
                  + jnp.dot(a_ref[...], wa_ref[...], preferred_element_type=F32)
                  + jnp.dot(p_ref[...], wp_ref[...], preferred_element_type=F32))


def _outproj(attn, pool, w_out, x2):
    T, D = x2.shape
    tm, tn = 512, 1024
    ka, kp = attn.shape[1], pool.shape[1]
    assert ka == kp
    return pl.pallas_call(
        _outproj_kernel,
        grid=(T // tm, D // tn),
        in_specs=[
            pl.BlockSpec((tm, ka), lambda i, j: (i, 0)),
            pl.BlockSpec((tm, kp), lambda i, j: (i, 0)),
            pl.BlockSpec((ka, tn), lambda i, j: (0, j)),
            pl.BlockSpec((kp, tn), lambda i, j: (1, j)),
            pl.BlockSpec((tm, tn), lambda i, j: (i, j)),
        ],
        out_specs=pl.BlockSpec((tm, tn), lambda i, j: (i, j)),
        out_shape=jax.ShapeDtypeStruct((T, D), F32),
        compiler_params=_params(("parallel", "parallel")),
        name="outproj",
    )(attn, pool, w_out, w_out, x2)


def _gateup_kernel(x_ref, g_ref, wg_ref, wu_ref, a_ref, h_ref):
    @pl.when(pl.program_id(1) == 0)
    def _():
        h_ref[...] = _rmsnorm_bf16(x_ref[...], g_ref[...])

    h = h_ref[...]
    gate = jnp.dot(h, wg_ref[...], preferred_element_type=F32)
    up = jnp.dot(h, wu_ref[...], preferred_element_type=F32)
    a_ref[...] = (gate * (1.0 / (1.0 + jnp.exp(-gate))) * up).astype(BF16)


def _gateup(x1, g, w_gate, w_up):
    T, D = x1.shape
    F = w_gate.shape[1]
    tm, tn = 512, 512
    return pl.pallas_call(
        _gateup_kernel,
        grid=(T // tm, pl.cdiv(F, tn)),
        in_specs=[
            pl.BlockSpec((tm, D), lambda i, j: (i, 0)),
            pl.BlockSpec((1, D), lambda i, j: (0, 0)),
            pl.BlockSpec((D, tn), lambda i, j: (0, j)),
            pl.BlockSpec((D, tn), lambda i, j: (0, j)),
        ],
        out_specs=pl.BlockSpec((tm, tn), lambda i, j: (i, j)),
        out_shape=jax.ShapeDtypeStruct((T, F), BF16),
        scratch_shapes=[pltpu.VMEM((tm, D), BF16)],
        compiler_params=_params(("arbitrary", "arbitrary")),
        name="ffn_gateup",
    )(x1, g, w_gate, w_up)


def _down_kernel(a_ref, w_ref, x_ref, o_ref, acc_ref):
    k = pl.program_id(2)

    @pl.when(k == 0)
    def _():
        acc_ref[...] = x_ref[...]

    acc_ref[...] += jnp.dot(a_ref[...], w_ref[...], preferred_element_type=F32)

    @pl.when(k == pl.num_programs(2) - 1)
    def _():
        o_ref[...] = acc_ref[...]


def _down(a, w_down, x1):
    T, D = x1.shape
    F = a.shape[1]
    tm, tn = 512, 1024
    tk = F // 2 if (F // 2) % LANES == 0 else F
    return pl.pallas_call(
        _down_kernel,
        grid=(T // tm, D // tn, F // tk),
        in_specs=[
            pl.BlockSpec((tm, tk), lambda i, j, k: (i, k)),
            pl.BlockSpec((tk, tn), lambda i, j, k: (k, j)),
            pl.BlockSpec((tm, tn), lambda i, j, k: (i, j)),
        ],
        out_specs=pl.BlockSpec((tm, tn), lambda i, j, k: (i, j)),
        out_shape=jax.ShapeDtypeStruct((T, D), F32),
        scratch_shapes=[pltpu.VMEM((tm, tn), F32)],
        compiler_params=_params(("parallel", "parallel", "arbitrary")),
        name="ffn_down",
    )(a, w_down, x1)


def _final_norm_kernel(x_ref, g_ref, o_ref):
    x = x_ref[...]
    ms = jnp.mean(x * x, axis=-1, keepdims=True)
    o_ref[...] = x * lax.rsqrt(ms + EPS) * g_ref[...]


def _final_norm(x2, g):
    T, D = x2.shape
    tm = 256
    return pl.pallas_call(
        _final_norm_kernel,
        grid=(T // tm,),
        in_specs=[pl.BlockSpec((tm, D), lambda i: (i, 0)),
                  pl.BlockSpec((1, D), lambda i: (0, 0))],
        out_specs=pl.BlockSpec((tm, D), lambda i: (i, 0)),
        out_shape=jax.ShapeDtypeStruct((T, D), F32),
        compiler_params=_params(("parallel",)),
        name="final_norm",
    )(x2, g)


def _rope_table(seq, head_dim):
    rot = head_dim // ROT_FRACTION
    half = rot // 2
    inv_freq = ROPE_THETA ** (-jnp.arange(0, rot, 2, dtype=F32) / rot)
    ang = jnp.arange(seq).astype(F32)[:, None] * inv_freq[None, :]
    cos, sin = jnp.cos(ang), jnp.sin(ang)
    feature_major = jnp.stack([cos.T, sin.T])
    zeros = lambda n: jnp.zeros((seq, n), F32)
    c = jnp.concatenate([cos, cos, jnp.ones((seq, head_dim - rot), F32)], axis=1)
    sa = jnp.concatenate([-sin, zeros(head_dim - half)], axis=1)
    sb = jnp.concatenate([zeros(half), sin, zeros(head_dim - rot)], axis=1)
    reps = LANES // head_dim
    return jnp.stack([jnp.tile(t, (1, reps)) for t in (c, sa, sb)]), feature_major


def kernel(x, norm_mix_g, w_in, w_pool, pool_scale, w_out, norm_ffn_g, w_gate, w_up, w_down,
           norm_final_g):
    B, S, D = x.shape
    T = B * S
    depth = w_in.shape[0]
    pool_w = pool_scale.shape[1]
    small0 = ATTN_WIDTH + 2 * KV_WIDTH + IDX_WIDTH
    small1 = small0 + IDX_DIM + IDX_HEADS

    rope_a, rope_ta = _rope_table(S, HEAD_DIM)
    rope_i, rope_ti = _rope_table(S, IDX_DIM)
    x2 = x.reshape(T, D)
    for l in range(depth):
        wl = w_in[l].astype(BF16)
        w_main = jnp.concatenate([wl[:, :small0], wl[:, small1:]], axis=1)
        w_small = jnp.concatenate(
            [wl[:, small0:small0 + IDX_DIM], jnp.zeros((D, LANES - IDX_DIM), BF16),
             wl[:, small0 + IDX_DIM:small1], jnp.zeros((D, LANES - IDX_HEADS), BF16)],
            axis=1)

        qT, kr, vT, iqT, ikp, iwT, u = _inproj(
            x2, norm_mix_g[l][None], w_main, w_small, rope_a, rope_i, rope_ta, rope_ti, S)
        pool = _pool(u, w_pool[l].astype(BF16), pool_scale[l][None], B, S)
        attn = _attention(qT, iqT, iwT, kr, ikp, vT, B, S)
        x2 = _outproj(attn, pool, w_out[l].astype(BF16), x2)
        a = _gateup(x2, norm_ffn_g[l][None], w_gate[l].astype(BF16), w_up[l].astype(BF16))
        x2 = _down(a, w_down[l].astype(BF16), x2)
    return _final_norm(x2, norm_final_g[None]).reshape(B, S, D)
```

```python
import functools

import jax
import jax.numpy as jnp
import numpy as np
from jax import lax
from jax.experimental import pallas as pl
from jax.experimental.pallas import tpu as pltpu

N_HEADS = 16
N_KV_HEADS = 4
HEAD_DIM = 128
GROUP = N_HEADS // N_KV_HEADS
ATTN_WIDTH = N_HEADS * HEAD_DIM
KV_WIDTH = N_KV_HEADS * HEAD_DIM
IDX_HEADS = 16
IDX_DIM = 64
IDX_WIDTH = IDX_HEADS * IDX_DIM
TOPK_MAX = 256
POOL_WINDOWS = (2, 4, 8, 16)
MAX_WINDOW = max(POOL_WINDOWS)
ROPE_THETA = 500000.0
ROT_FRACTION = 4
EPS = 1e-6

LANES = 128
VMEM_LIMIT = 56 * 1024 * 1024

TQ = 128
TK = 512
V_ROWS = HEAD_DIM + 16
MASK_NEG = -1e30

I16_MIN = -2 ** 15
PACK = 16
LOG2_E = 1.4426950408889634
NEG_INF_KEY = int(np.array(-np.inf, np.float32).view(np.int32)) ^ 0x7FFFFFFF
MASK_NEG_BITS = int(np.array(MASK_NEG, np.float32).view(np.int32))

BF16 = jnp.bfloat16
F32 = jnp.float32
I32 = jnp.int32
I16 = jnp.int16


def _params(semantics):
    return pltpu.CompilerParams(dimension_semantics=semantics, vmem_limit_bytes=VMEM_LIMIT)


def _rope(z, tab_ref, half):
    return (z * tab_ref[0]
            + pltpu.roll(z, LANES - half, 1) * tab_ref[1]
            + pltpu.roll(z, half, 1) * tab_ref[2])


def _rope_rows(zt, tab_ref, head_dim):
    half = head_dim // ROT_FRACTION // 2
    c, s = tab_ref[0], tab_ref[1]
    parts = []
    for base in range(0, LANES, head_dim):
        x1 = zt[base:base + half]
        x2 = zt[base + half:base + 2 * half]
        parts += [x1 * c - x2 * s, x2 * c + x1 * s, zt[base + 2 * half:base + head_dim]]
    return jnp.concatenate(parts, axis=0)


def _rmsnorm_bf16(x, g):
    ms = jnp.mean(x * x, axis=-1, keepdims=True)
    return (x * lax.rsqrt(ms + EPS) * g).astype(BF16)


def _inproj_kernel(x_ref, g_ref, wm_ref, ws_ref, ra_ref, ri_ref, rta_ref, rti_ref,
                   qT_ref, kr_ref, vT_ref, iqT_ref, ikp_ref, iwT_ref, u_ref, h_ref):
    j = pl.program_id(1)
    q_scale = HEAD_DIM ** -0.5 * LOG2_E
    idx_scale = (IDX_DIM ** -0.5) * (IDX_HEADS ** -0.5)
    a_half = HEAD_DIM // ROT_FRACTION // 2
    i_half = IDX_DIM // ROT_FRACTION // 2

    @pl.when(j == 0)
    def _():
        h_ref[...] = _rmsnorm_bf16(x_ref[...], g_ref[...])
        zs = jnp.dot(h_ref[...], ws_ref[...], preferred_element_type=F32)
        ikp_ref[...] = _rope(zs[:, :LANES], ri_ref, i_half).astype(BF16)
        iwT_ref[...] = (zs[:, LANES:] * idx_scale).T

    z = jnp.dot(h_ref[...], wm_ref[...], preferred_element_type=F32)
    n_tiles = z.shape[1] // LANES

    @pl.when(j <= 1)
    def _():
        for t in range(n_tiles):
            zt = (z[:, t * LANES:(t + 1) * LANES] * q_scale).T
            qT_ref[t * LANES:(t + 1) * LANES, :] = _rope_rows(zt, rta_ref, HEAD_DIM).astype(BF16)

    @pl.when(j == 2)
    def _():
        for t in range(N_KV_HEADS):
            zz = _rope(z[:, t * LANES:(t + 1) * LANES], ra_ref, a_half)
            kr_ref[:, t * LANES:(t + 1) * LANES] = zz.astype(BF16)
        for t in range(N_KV_HEADS):
            c0 = KV_WIDTH + t * LANES
            vT_ref[t * LANES:(t + 1) * LANES, :] = z[:, c0:c0 + LANES].T.astype(BF16)

    @pl.when(j == 3)
    def _():
        for t in range(n_tiles):
            zt = z[:, t * LANES:(t + 1) * LANES].T
            iqT_ref[t * LANES:(t + 1) * LANES, :] = _rope_rows(zt, rti_ref, IDX_DIM).astype(BF16)

    @pl.when(j >= 4)
    def _():
        u_ref[...] = z


def _inproj(x2, g, w_main, w_small, rope_a, rope_i, rope_ta, rope_ti, seq):
    T, D = x2.shape
    tm = min(512, seq)
    tn = 1024
    nj = w_main.shape[1] // tn
    pool_w = w_main.shape[1] - 4 * tn
    sb = seq // tm
    out_shape = (
        jax.ShapeDtypeStruct((ATTN_WIDTH, T), BF16),
        jax.ShapeDtypeStruct((T, KV_WIDTH), BF16),
        jax.ShapeDtypeStruct((KV_WIDTH, T), BF16),
        jax.ShapeDtypeStruct((IDX_WIDTH, T), BF16),
        jax.ShapeDtypeStruct((T, LANES), BF16),
        jax.ShapeDtypeStruct((LANES, T), F32),
        jax.ShapeDtypeStruct((T, pool_w), F32),
    )
    return pl.pallas_call(
        _inproj_kernel,
        grid=(T // tm, nj),
        in_specs=[
            pl.BlockSpec((tm, D), lambda i, j: (i, 0), pipeline_mode=pl.Buffered(1)),
            pl.BlockSpec((1, D), lambda i, j: (0, 0)),
            pl.BlockSpec((D, tn), lambda i, j: (0, j)),
            pl.BlockSpec((D, 2 * LANES), lambda i, j: (0, 0), pipeline_mode=pl.Buffered(1)),
            pl.BlockSpec((3, tm, LANES), lambda i, j: (0, i % sb, 0)),
            pl.BlockSpec((3, tm, LANES), lambda i, j: (0, i % sb, 0)),
            pl.BlockSpec((2, rope_ta.shape[1], tm), lambda i, j: (0, 0, i % sb)),
            pl.BlockSpec((2, rope_ti.shape[1], tm), lambda i, j: (0, 0, i % sb)),
        ],
        out_specs=(
            pl.BlockSpec((tn, tm), lambda i, j: (jnp.minimum(j, 1), i)),
            pl.BlockSpec((tm, KV_WIDTH), lambda i, j: (i, 0)),
            pl.BlockSpec((KV_WIDTH, tm), lambda i, j: (0, i)),
            pl.BlockSpec((IDX_WIDTH, tm), lambda i, j: (0, i)),
            pl.BlockSpec((tm, LANES), lambda i, j: (i, 0)),
            pl.BlockSpec((LANES, tm), lambda i, j: (0, i)),
            pl.BlockSpec((tm, tn), lambda i, j: (i, jnp.clip(j - 4, 0, pool_w // tn - 1))),
        ),
        out_shape=out_shape,
        scratch_shapes=[pltpu.VMEM((tm, D), BF16)],
        compiler_params=_params(("arbitrary", "arbitrary")),
        name="inproj",
    )(x2, g, w_main, w_small, rope_a, rope_i, rope_ta, rope_ti)


def _pool_kernel(u_ref, wp_ref, ps_ref, o_ref, ext_ref):
    s = pl.program_id(1)
    ts = u_ref.shape[0]
    cg = wp_ref.shape[1]

    @pl.when(s == 0)
    def _():
        ext_ref[0:MAX_WINDOW, :] = jnp.zeros((MAX_WINDOW, ext_ref.shape[1]), F32)

    @pl.when(s > 0)
    def _():
        ext_ref[0:MAX_WINDOW, :] = ext_ref[ts:ts + MAX_WINDOW, :]

    ext_ref[MAX_WINDOW:MAX_WINDOW + ts, :] = u_ref[...]
    t = s * ts + lax.broadcasted_iota(I32, (ts, 1), 0)
    for g, w in enumerate(POOL_WINDOWS):
        c0 = g * cg
        cur = ext_ref[MAX_WINDOW:MAX_WINDOW + ts, c0:c0 + cg]
        acc = cur
        for back in range(1, w):
            acc = acc + ext_ref[MAX_WINDOW - back:MAX_WINDOW - back + ts, c0:c0 + cg]
        cnt = jnp.minimum(t + 1, w).astype(F32)
        diff = (acc / cnt - cur).astype(BF16)
        mixed = jnp.dot(diff, wp_ref[g], preferred_element_type=F32)
        o_ref[:, c0:c0 + cg] = (mixed * ps_ref[:, c0:c0 + cg]).astype(BF16)


def _pool(u, w_pool, pool_scale, batch, seq):
    T, C = u.shape
    ts = min(512, seq)
    ns = seq // ts
    G, cg, _ = w_pool.shape
    return pl.pallas_call(
        _pool_kernel,
        grid=(batch, ns),
        in_specs=[
            pl.BlockSpec((ts, C), lambda b, s: (b * ns + s, 0)),
            pl.BlockSpec((G, cg, cg), lambda b, s: (0, 0, 0)),
            pl.BlockSpec((1, C), lambda b, s: (0, 0)),
        ],
        out_specs=pl.BlockSpec((ts, C), lambda b, s: (b * ns + s, 0)),
        out_shape=jax.ShapeDtypeStruct((T, C), BF16),
        scratch_shapes=[pltpu.VMEM((ts + MAX_WINDOW, C), F32)],
        compiler_params=_params(("arbitrary", "arbitrary")),
        name="pool",
    )(u, w_pool, pool_scale)


def _attn_kernel(qT_ref, iqT_ref, iwT_ref, kr_ref, ikp_ref, vT_ref, o_ref,
                 key_ref, hi_ref, lo_ref, vte_ref, cut_ref, qg_ref, m_ref, acc_ref,
                 s0_ref, s1_ref, mt0_ref, mt1_ref, *, seq, topk):
    i = pl.program_id(1)
    n_tiles = (i * TQ + TQ + TK - 1) // TK
    row_bits = (seq - 1).bit_length()

    @pl.when(i == 0)
    def _():
        for n in range(N_KV_HEADS):
            vte_ref[n, 0:HEAD_DIM, :] = vT_ref[n * HEAD_DIM:(n + 1) * HEAD_DIM, :]
            vte_ref[n, HEAD_DIM:V_ROWS, :] = jnp.ones((V_ROWS - HEAD_DIM, seq), BF16)

    qpos = i * TQ + lax.broadcasted_iota(I32, (1, TQ), 1)

    def tile_rows(t):
        r0 = pl.multiple_of(t * TK, TK)
        return r0, r0 + lax.broadcasted_iota(I32, (TK, TQ), 0)

    iq = iqT_ref[...]
    idx_rhs = jnp.concatenate(
        [iq[h * IDX_DIM:(h + 1) * IDX_DIM, :] for h in range(IDX_HEADS)], axis=1)
    idx_rhs = jnp.concatenate([idx_rhs, jnp.zeros_like(idx_rhs)], axis=0)
    iw = iwT_ref[...]

    def score_tile(t):
        r0, rows = tile_rows(t)
        logits = jnp.dot(ikp_ref[pl.ds(r0, TK), :], idx_rhs, preferred_element_type=F32)
        sc = jnp.zeros((TK, TQ), F32)
        for h in range(IDX_HEADS):
            sc = sc + jnp.maximum(logits[:, h * TQ:(h + 1) * TQ], 0.0) * iw[h:h + 1, :]
        sc = jnp.where(rows <= qpos, sc + 0.0, -jnp.inf)
        bits = pltpu.bitcast(sc, I32)
        key = jnp.where(bits < 0, bits ^ 0x7FFFFFFF, bits)
        key_ref[pl.ds(r0, TK), :] = key
        hi_ref[pl.ds(r0, TK), :] = (key >> 16).astype(I16)
        lo_ref[pl.ds(r0, TK), :] = (((key ^ 0x8000) << 16) >> 16).astype(I16)

    n_pairs = (n_tiles + 1) // 2

    def score_body(tt, carry):
        score_tile(2 * tt)
        score_tile(2 * tt + 1)
        return carry

    lax.fori_loop(0, n_pairs, score_body, 0)

    def rows16(v):
        return jnp.broadcast_to(v, (PACK, TQ)).astype(I16)

    def count16(ref, cand, op):
        c16 = rows16(cand)

        def body(t, accs):
            v = ref[pl.ds(pl.multiple_of(t * 2 * TK, 2 * TK), 2 * TK), :]
            accs = list(accs)
            for r in range(2 * TK // PACK):
                a = r % len(accs)
                hit = op(v[r * PACK:(r + 1) * PACK, :], c16)
                accs[a] = accs[a] + jnp.where(hit, jnp.int16(1), jnp.int16(0))
            return tuple(accs)

        zero = jnp.zeros((PACK, TQ), I16)
        accs = lax.fori_loop(0, n_pairs, body, (zero,) * 4)
        total = sum(a.astype(I32) for a in accs)
        return jnp.sum(total, axis=0, keepdims=True)

    ge = lambda v, c: v >= c
    gt = lambda v, c: v > c

    def half_search(ref, base, count_all):
        def bit(p, carry):
            lo, cnt_lo = carry
            cand = lo + jnp.left_shift(jnp.int32(1), 15 - p)
            cnt = base + count16(ref, cand, ge)
            ok = cnt >= topk
            return jnp.where(ok, cand, lo), jnp.where(ok, cnt, cnt_lo)
        return lax.fori_loop(0, 16, bit, (jnp.full((1, TQ), I16_MIN, I32), count_all))

    tau_hi, c_ge_hi = half_search(hi_ref, 0, jnp.broadcast_to(n_pairs * 2 * TK, (1, TQ)))
    c_gt_hi = count16(hi_ref, tau_hi, gt)
    tau_hi16 = rows16(tau_hi)

    def narrow_body(t, carry):
        sl = pl.ds(pl.multiple_of(t * 2 * TK, 2 * TK), 2 * TK)
        hi, lo = hi_ref[sl, :], lo_ref[sl, :]
        lo_ref[sl, :] = jnp.concatenate(
            [jnp.where(hi[r * PACK:(r + 1) * PACK, :] == tau_hi16,
                       lo[r * PACK:(r + 1) * PACK, :], jnp.int16(I16_MIN))
             for r in range(2 * TK // PACK)], axis=0)
        return carry

    lax.fori_loop(0, n_pairs, narrow_body, 0)
    tau_lo, c_ge = half_search(lo_ref, c_gt_hi, c_ge_hi)
    tau = tau_hi * 65536 + (tau_lo - I16_MIN)

    def count(pred):
        def body(t, c8):
            r0, rows = tile_rows(t)
            m = pred(key_ref[pl.ds(r0, TK), :], rows).astype(I32)
            return c8 + jnp.sum(m.reshape(TK // 8, 8, TQ), axis=0)
        c8 = lax.fori_loop(0, n_tiles, body, jnp.zeros((8, TQ), I32))
        return jnp.sum(c8, axis=0, keepdims=True)

    cut_ref[...] = jnp.full((1, TQ), seq, I32)
    tied = jnp.logical_and(c_ge > topk, tau > NEG_INF_KEY)

    @pl.when(jnp.max(tied.astype(I32)) > 0)
    def _():
        need = topk - count(lambda k, rows: k > tau)

        def row_bit(p, x):
            cand = x + jnp.left_shift(jnp.int32(1), row_bits - 1 - p)
            cnt = count(lambda k, rows: jnp.logical_and(k == tau, rows < cand))
            return jnp.where(cnt < need, cand, x)
        cut_ref[...] = lax.fori_loop(0, row_bits, row_bit, jnp.zeros((1, TQ), I32))

    cut = cut_ref[...]

    def mask_body(t, carry):
        r0, rows = tile_rows(t)
        k = key_ref[pl.ds(r0, TK), :]
        at_cut = jnp.where(rows <= cut, 0, MASK_NEG_BITS)
        keep = jnp.where(k > tau, 0, jnp.where(k == tau, at_cut, MASK_NEG_BITS))
        key_ref[pl.ds(r0, TK), :] = jnp.where(rows <= qpos, keep, MASK_NEG_BITS).astype(I32)
        return carry

    lax.fori_loop(0, n_tiles, mask_body, 0)

    for n in range(N_KV_HEADS):
        qg_ref[n] = jnp.concatenate(
            [qT_ref[(GROUP * n + g) * HEAD_DIM:(GROUP * n + g + 1) * HEAD_DIM, :]
             for g in range(GROUP)], axis=1)
        m_ref[n] = jnp.full((1, GROUP * TQ), MASK_NEG, F32)
        acc_ref[n] = jnp.zeros((V_ROWS, GROUP * TQ), F32)

    def score_stage(t, s_ref, mt_ref):
        r0 = pl.multiple_of(jnp.minimum(t, n_tiles - 1) * TK, TK)
        bias = pltpu.bitcast(key_ref[pl.ds(r0, TK), :], F32)
        bias = jnp.where(t < n_tiles, bias, MASK_NEG)
        bias = jnp.concatenate([bias] * GROUP, axis=1)
        for n in range(N_KV_HEADS):
            s = jnp.dot(kr_ref[pl.ds(r0, TK), n * HEAD_DIM:(n + 1) * HEAD_DIM], qg_ref[n],
                        preferred_element_type=F32) + bias
            s_ref[n] = s
            mt_ref[n] = jnp.max(s, axis=0, keepdims=True)

    def value_stage(t, s_ref, mt_ref):
        r0 = pl.multiple_of(jnp.minimum(t, n_tiles - 1) * TK, TK)
        for n in range(N_KV_HEADS):
            m = m_ref[n]
            m_new = jnp.maximum(m, mt_ref[n])
            p = jnp.exp2(s_ref[n] - m_new).astype(BF16)
            pv = jnp.dot(vte_ref[n, :, pl.ds(r0, TK)], p, preferred_element_type=F32)
            acc_ref[n] = acc_ref[n] * jnp.exp2(m - m_new) + pv
            m_ref[n] = m_new

    score_stage(0, s0_ref, mt0_ref)

    def attn_body(tt, carry):
        t = 2 * tt
        score_stage(t + 1, s1_ref, mt1_ref)
        value_stage(t, s0_ref, mt0_ref)
        score_stage(t + 2, s0_ref, mt0_ref)
        value_stage(t + 1, s1_ref, mt1_ref)
        return carry

    lax.fori_loop(0, (n_tiles + 1) // 2, attn_body, 0)

    for n in range(N_KV_HEADS):
        acc = acc_ref[n]
        o = acc[0:HEAD_DIM, :] / acc[HEAD_DIM:HEAD_DIM + 1, :]
        for g in range(GROUP):
            c0 = (GROUP * n + g) * HEAD_DIM
            o_ref[:, c0:c0 + HEAD_DIM] = o[:, g * TQ:(g + 1) * TQ].T.astype(BF16)


def _attention(qT, iqT, iwT, kr, ikp, vT, batch, seq):
    T = kr.shape[0]
    nq = seq // TQ
    topk = min(TOPK_MAX, seq // 4)
    assert seq % (2 * TK) == 0 and seq % TQ == 0
    once = pl.Buffered(1)
    return pl.pallas_call(
        functools.partial(_attn_kernel, seq=seq, topk=topk),
        grid=(batch, nq),
        in_specs=[
            pl.BlockSpec((ATTN_WIDTH, TQ), lambda b, i: (0, b * nq + i)),
            pl.BlockSpec((IDX_WIDTH, TQ), lambda b, i: (0, b * nq + i)),
            pl.BlockSpec((LANES, TQ), lambda b, i: (0, b * nq + i)),
            pl.BlockSpec((seq, KV_WIDTH), lambda b, i: (b, 0), pipeline_mode=once),
            pl.BlockSpec((seq, LANES), lambda b, i: (b, 0), pipeline_mode=once),
            pl.BlockSpec((KV_WIDTH, seq), lambda b, i: (0, b), pipeline_mode=once),
        ],
        out_specs=pl.BlockSpec((TQ, ATTN_WIDTH), lambda b, i: (b * nq + i, 0)),
        out_shape=jax.ShapeDtypeStruct((T, ATTN_WIDTH), BF16),
        scratch_shapes=[
            pltpu.VMEM((seq, TQ), I32),
            pltpu.VMEM((seq, TQ), I16),
            pltpu.VMEM((seq, TQ), I16),
            pltpu.VMEM((N_KV_HEADS, V_ROWS, seq), BF16),
            pltpu.VMEM((1, TQ), I32),
            pltpu.VMEM((N_KV_HEADS, HEAD_DIM, GROUP * TQ), BF16),
            pltpu.VMEM((N_KV_HEADS, 1, GROUP * TQ), F32),
            pltpu.VMEM((N_KV_HEADS, V_ROWS, GROUP * TQ), F32),
            pltpu.VMEM((N_KV_HEADS, TK, GROUP * TQ), F32),
            pltpu.VMEM((N_KV_HEADS, TK, GROUP * TQ), F32),
            pltpu.VMEM((N_KV_HEADS, 1, GROUP * TQ), F32),
            pltpu.VMEM((N_KV_HEADS, 1, GROUP * TQ), F32),
        ],
        compiler_params=_params(("arbitrary", "arbitrary")),
        name="sparse_attn",
    )(qT, iqT, iwT, kr, ikp, vT)


def _outproj_kernel(a_ref, p_ref, wa_ref, wp_ref, x_ref, o_ref):
    o_ref[...] = (x_ref[...]
                  + jnp.dot(a_ref[...], wa_ref[...], preferred_element_type=F32)
                  + jnp.dot(p_ref[...], wp_ref[...], preferred_element_type=F32))


def _outproj(attn, pool, w_out, x2):
    T, D = x2.shape
    tm, tn = 512, 1024
    ka, kp = attn.shape[1], pool.shape[1]
    assert ka == kp
    return pl.pallas_call(
        _outproj_kernel,
        grid=(T // tm, D // tn),
        in_specs=[
            pl.BlockSpec((tm, ka), lambda i, j: (i, 0)),
            pl.BlockSpec((tm, kp), lambda i, j: (i, 0)),
            pl.BlockSpec((ka, tn), lambda i, j: (0, j)),
            pl.BlockSpec((kp, tn), lambda i, j: (1, j)),
            pl.BlockSpec((tm, tn), lambda i, j: (i, j)),
        ],
        out_specs=pl.BlockSpec((tm, tn), lambda i, j: (i, j)),
        out_shape=jax.ShapeDtypeStruct((T, D), F32),
        compiler_params=_params(("parallel", "parallel")),
        name="outproj",
    )(attn, pool, w_out, w_out, x2)


def _gateup_kernel(x_ref, g_ref, wg_ref, wu_ref, a_ref, h_ref):
    @pl.when(pl.program_id(1) == 0)
    def _():
        h_ref[...] = _rmsnorm_bf16(x_ref[...], g_ref[...])

    h = h_ref[...]
    gate = jnp.dot(h, wg_ref[...], preferred_element_type=F32)
    up = jnp.dot(h, wu_ref[...], preferred_element_type=F32)
    a_ref[...] = (gate * (1.0 / (1.0 + jnp.exp(-gate))) * up).astype(BF16)


def _gateup(x1, g, w_gate, w_up):
    T, D = x1.shape
    F = w_gate.shape[1]
    tm, tn = 512, 512
    return pl.pallas_call(
        _gateup_kernel,
        grid=(T // tm, pl.cdiv(F, tn)),
        in_specs=[
            pl.BlockSpec((tm, D), lambda i, j: (i, 0)),
            pl.BlockSpec((1, D), lambda i, j: (0, 0)),
            pl.BlockSpec((D, tn), lambda i, j: (0, j)),
            pl.BlockSpec((D, tn), lambda i, j: (0, j)),
        ],
        out_specs=pl.BlockSpec((tm, tn), lambda i, j: (i, j)),
        out_shape=jax.ShapeDtypeStruct((T, F), BF16),
        scratch_shapes=[pltpu.VMEM((tm, D), BF16)],
        compiler_params=_params(("arbitrary", "arbitrary")),
        name="ffn_gateup",
    )(x1, g, w_gate, w_up)


def _down_kernel(a_ref, w_ref, x_ref, o_ref, acc_ref):
    k = pl.program_id(2)

    @pl.when(k == 0)
    def _():
        acc_ref[...] = x_ref[...]

    acc_ref[...] += jnp.dot(a_ref[...], w_ref[...], preferred_element_type=F32)

    @pl.when(k == pl.num_programs(2) - 1)
    def _():
        o_ref[...] = acc_ref[...]


def _down(a, w_down, x1):
    T, D = x1.shape
    F = a.shape[1]
    tm, tn = 512, 1024
    tk = F // 2 if (F // 2) % LANES == 0 else F
    return pl.pallas_call(
        _down_kernel,
        grid=(T // tm, D // tn, F // tk),
        in_specs=[
            pl.BlockSpec((tm, tk), lambda i, j, k: (i, k)),
            pl.BlockSpec((tk, tn), lambda i, j, k: (k, j)),
            pl.BlockSpec((tm, tn), lambda i, j, k: (i, j)),
        ],
        out_specs=pl.BlockSpec((tm, tn), lambda i, j, k: (i, j)),
        out_shape=jax.ShapeDtypeStruct((T, D), F32),
        scratch_shapes=[pltpu.VMEM((tm, tn), F32)],
        compiler_params=_params(("parallel", "parallel", "arbitrary")),
        name="ffn_down",
    )(a, w_down, x1)


def _final_norm_kernel(x_ref, g_ref, o_ref):
    x = x_ref[...]
    ms = jnp.mean(x * x, axis=-1, keepdims=True)
    o_ref[...] = x * lax.rsqrt(ms + EPS) * g_ref[...]


def _final_norm(x2, g):
    T, D = x2.shape
    tm = 256
    return pl.pallas_call(
        _final_norm_kernel,
        grid=(T // tm,),
        in_specs=[pl.BlockSpec((tm, D), lambda i: (i, 0)),
                  pl.BlockSpec((1, D), lambda i: (0, 0))],
        out_specs=pl.BlockSpec((tm, D), lambda i: (i, 0)),
        out_shape=jax.ShapeDtypeStruct((T, D), F32),
        compiler_params=_params(("parallel",)),
        name="final_norm",
    )(x2, g)


def _rope_table(seq, head_dim):
    rot = head_dim // ROT_FRACTION
    half = rot // 2
    inv_freq = ROPE_THETA ** (-jnp.arange(0, rot, 2, dtype=F32) / rot)
    ang = jnp.arange(seq).astype(F32)[:, None] * inv_freq[None, :]
    cos, sin = jnp.cos(ang), jnp.sin(ang)
    feature_major = jnp.stack([cos.T, sin.T])
    zeros = lambda n: jnp.zeros((seq, n), F32)
    c = jnp.concatenate([cos, cos, jnp.ones((seq, head_dim - rot), F32)], axis=1)
    sa = jnp.concatenate([-sin, zeros(head_dim - half)], axis=1)
    sb = jnp.concatenate([zeros(half), sin, zeros(head_dim - rot)], axis=1)
    reps = LANES // head_dim
    return jnp.stack([jnp.tile(t, (1, reps)) for t in (c, sa, sb)]), feature_major


def kernel(x, norm_mix_g, w_in, w_pool, pool_scale, w_out, norm_ffn_g, w_gate, w_up, w_down,
           norm_final_g):
    B, S, D = x.shape
    T = B * S
    depth = w_in.shape[0]
    pool_w = pool_scale.shape[1]
    small0 = ATTN_WIDTH + 2 * KV_WIDTH + IDX_WIDTH
    small1 = small0 + IDX_DIM + IDX_HEADS

    rope_a, rope_ta = _rope_table(S, HEAD_DIM)
    rope_i, rope_ti = _rope_table(S, IDX_DIM)
    x2 = x.reshape(T, D)
    for l in range(depth):
        wl = w_in[l].astype(BF16)
        w_main = jnp.concatenate([wl[:, :small0], wl[:, small1:]], axis=1)
        w_small = jnp.concatenate(
            [wl[:, small0:small0 + IDX_DIM], jnp.zeros((D, LANES - IDX_DIM), BF16),
             wl[:, small0 + IDX_DIM:small1], jnp.zeros((D, LANES - IDX_HEADS), BF16)],
            axis=1)

        qT, kr, vT, iqT, ikp, iwT, u = _inproj(
            x2, norm_mix_g[l][None], w_main, w_small, rope_a, rope_i, rope_ta, rope_ti, S)
        pool = _pool(u, w_pool[l].astype(BF16), pool_scale[l][None], B, S)
        attn = _attention(qT, iqT, iwT, kr, ikp, vT, B, S)
        x2 = _outproj(attn, pool, w_out[l].astype(BF16), x2)
        a = _gateup(x2, norm_ffn_g[l][None], w_gate[l].astype(BF16), w_up[l].astype(BF16))
        x2 = _down(a, w_down[l].astype(BF16), x2)
    return _final_norm(x2, norm_final_g[None]).reshape(B, S, D)
```

```python
import functools

import jax
import jax.numpy as jnp
import numpy as np
from jax import lax
from jax.experimental import pallas as pl
from jax.experimental.pallas import tpu as pltpu

N_HEADS = 16
N_KV_HEADS = 4
HEAD_DIM = 128
GROUP = N_HEADS // N_KV_HEADS
ATTN_WIDTH = N_HEADS * HEAD_DIM
KV_WIDTH = N_KV_HEADS * HEAD_DIM
IDX_HEADS = 16
IDX_DIM = 64
IDX_WIDTH = IDX_HEADS * IDX_DIM
TOPK_MAX = 256
POOL_WINDOWS = (2, 4, 8, 16)
MAX_WINDOW = max(POOL_WINDOWS)
ROPE_THETA = 500000.0
ROT_FRACTION = 4
EPS = 1e-6

LANES = 128
VMEM_LIMIT = 56 * 1024 * 1024

TQ = 128
TK = 512
V_ROWS = HEAD_DIM + 16
MASK_NEG = -1e30

I16_MIN = -2 ** 15
PACK = 16
LOG2_E = 1.4426950408889634
NEG_INF_KEY = int(np.array(-np.inf, np.float32).view(np.int32)) ^ 0x7FFFFFFF
MASK_NEG_BITS = int(np.array(MASK_NEG, np.float32).view(np.int32))

BF16 = jnp.bfloat16
F32 = jnp.float32
I32 = jnp.int32
I16 = jnp.int16


def _params(semantics):
    return pltpu.CompilerParams(dimension_semantics=semantics, vmem_limit_bytes=VMEM_LIMIT)


def _rope(z, tab_ref, half):
    return (z * tab_ref[0]
            + pltpu.roll(z, LANES - half, 1) * tab_ref[1]
            + pltpu.roll(z, half, 1) * tab_ref[2])


def _rope_rows(zt, tab_ref, head_dim):
    half = head_dim // ROT_FRACTION // 2
    c, s = tab_ref[0], tab_ref[1]
    parts = []
    for base in range(0, LANES, head_dim):
        x1 = zt[base:base + half]
        x2 = zt[base + half:base + 2 * half]
        parts += [x1 * c - x2 * s, x2 * c + x1 * s, zt[base + 2 * half:base + head_dim]]
    return jnp.concatenate(parts, axis=0)


def _rmsnorm_bf16(x, g):
    ms = jnp.mean(x * x, axis=-1, keepdims=True)
    return (x * lax.rsqrt(ms + EPS) * g).astype(BF16)


def _inproj_kernel(x_ref, g_ref, wm_ref, ws_ref, ra_ref, ri_ref, rta_ref, rti_ref,
                   qT_ref, kr_ref, vT_ref, iqT_ref, ikp_ref, iwT_ref, u_ref, h_ref):
    j = pl.program_id(1)
    q_scale = HEAD_DIM ** -0.5 * LOG2_E
    idx_scale = (IDX_DIM ** -0.5) * (IDX_HEADS ** -0.5)
    a_half = HEAD_DIM // ROT_FRACTION // 2
    i_half = IDX_DIM // ROT_FRACTION // 2

    @pl.when(j == 0)
    def _():
        h_ref[...] = _rmsnorm_bf16(x_ref[...], g_ref[...])
        zs = jnp.dot(h_ref[...], ws_ref[...], preferred_element_type=F32)
        ikp_ref[...] = _rope(zs[:, :LANES], ri_ref, i_half).astype(BF16)
        iwT_ref[...] = (zs[:, LANES:] * idx_scale).T

    z = jnp.dot(h_ref[...], wm_ref[...], preferred_element_type=F32)
    n_tiles = z.shape[1] // LANES

    @pl.when(j <= 1)
    def _():
        for t in range(n_tiles):
            zt = (z[:, t * LANES:(t + 1) * LANES] * q_scale).T
            qT_ref[t * LANES:(t + 1) * LANES, :] = _rope_rows(zt, rta_ref, HEAD_DIM).astype(BF16)

    @pl.when(j == 2)
    def _():
        for t in range(N_KV_HEADS):
            zz = _rope(z[:, t * LANES:(t + 1) * LANES], ra_ref, a_half)
            kr_ref[:, t * LANES:(t + 1) * LANES] = zz.astype(BF16)
        for t in range(N_KV_HEADS):
            c0 = KV_WIDTH + t * LANES
            vT_ref[t * LANES:(t + 1) * LANES, :] = z[:, c0:c0 + LANES].T.astype(BF16)

    @pl.when(j == 3)
    def _():
        for t in range(n_tiles):
            zt = z[:, t * LANES:(t + 1) * LANES].T
            iqT_ref[t * LANES:(t + 1) * LANES, :] = _rope_rows(zt, rti_ref, IDX_DIM).astype(BF16)

    @pl.when(j >= 4)
    def _():
        u_ref[...] = z


def _inproj(x2, g, w_main, w_small, rope_a, rope_i, rope_ta, rope_ti, seq):
    T, D = x2.shape
    tm = min(512, seq)
    tn = 1024
    nj = w_main.shape[1] // tn
    pool_w = w_main.shape[1] - 4 * tn
    sb = seq // tm
    out_shape = (
        jax.ShapeDtypeStruct((ATTN_WIDTH, T), BF16),
        jax.ShapeDtypeStruct((T, KV_WIDTH), BF16),
        jax.ShapeDtypeStruct((KV_WIDTH, T), BF16),
        jax.ShapeDtypeStruct((IDX_WIDTH, T), BF16),
        jax.ShapeDtypeStruct((T, LANES), BF16),
        jax.ShapeDtypeStruct((LANES, T), F32),
        jax.ShapeDtypeStruct((T, pool_w), F32),
    )
    return pl.pallas_call(
        _inproj_kernel,
        grid=(T // tm, nj),
        in_specs=[
            pl.BlockSpec((tm, D), lambda i, j: (i, 0), pipeline_mode=pl.Buffered(1)),
            pl.BlockSpec((1, D), lambda i, j: (0, 0)),
            pl.BlockSpec((D, tn), lambda i, j: (0, j)),
            pl.BlockSpec((D, 2 * LANES), lambda i, j: (0, 0), pipeline_mode=pl.Buffered(1)),
            pl.BlockSpec((3, tm, LANES), lambda i, j: (0, i % sb, 0)),
            pl.BlockSpec((3, tm, LANES), lambda i, j: (0, i % sb, 0)),
            pl.BlockSpec((2, rope_ta.shape[1], tm), lambda i, j: (0, 0, i % sb)),
            pl.BlockSpec((2, rope_ti.shape[1], tm), lambda i, j: (0, 0, i % sb)),
        ],
        out_specs=(
            pl.BlockSpec((tn, tm), lambda i, j: (jnp.minimum(j, 1), i)),
            pl.BlockSpec((tm, KV_WIDTH), lambda i, j: (i, 0)),
            pl.BlockSpec((KV_WIDTH, tm), lambda i, j: (0, i)),
            pl.BlockSpec((IDX_WIDTH, tm), lambda i, j: (0, i)),
            pl.BlockSpec((tm, LANES), lambda i, j: (i, 0)),
            pl.BlockSpec((LANES, tm), lambda i, j: (0, i)),
            pl.BlockSpec((tm, tn), lambda i, j: (i, jnp.clip(j - 4, 0, pool_w // tn - 1))),
        ),
        out_shape=out_shape,
        scratch_shapes=[pltpu.VMEM((tm, D), BF16)],
        compiler_params=_params(("arbitrary", "arbitrary")),
        name="inproj",
    )(x2, g, w_main, w_small, rope_a, rope_i, rope_ta, rope_ti)


def _pool_kernel(u_ref, wp_ref, ps_ref, o_ref, ext_ref):
    s = pl.program_id(1)
    ts = u_ref.shape[0]
    cg = wp_ref.shape[1]

    @pl.when(s == 0)
    def _():
        ext_ref[0:MAX_WINDOW, :] = jnp.zeros((MAX_WINDOW, ext_ref.shape[1]), F32)

    @pl.when(s > 0)
    def _():
        ext_ref[0:MAX_WINDOW, :] = ext_ref[ts:ts + MAX_WINDOW, :]

    ext_ref[MAX_WINDOW:MAX_WINDOW + ts, :] = u_ref[...]
    t = s * ts + lax.broadcasted_iota(I32, (ts, 1), 0)
    for g, w in enumerate(POOL_WINDOWS):
        c0 = g * cg
        cur = ext_ref[MAX_WINDOW:MAX_WINDOW + ts, c0:c0 + cg]
        acc = cur
        for back in range(1, w):
            acc = acc + ext_ref[MAX_WINDOW - back:MAX_WINDOW - back + ts, c0:c0 + cg]
        cnt = jnp.minimum(t + 1, w).astype(F32)
        diff = (acc / cnt - cur).astype(BF16)
        mixed = jnp.dot(diff, wp_ref[g], preferred_element_type=F32)
        o_ref[:, c0:c0 + cg] = (mixed * ps_ref[:, c0:c0 + cg]).astype(BF16)


def _pool(u, w_pool, pool_scale, batch, seq):
    T, C = u.shape
    ts = min(512, seq)
    ns = seq // ts
    G, cg, _ = w_pool.shape
    return pl.pallas_call(
        _pool_kernel,
        grid=(batch, ns),
        in_specs=[
            pl.BlockSpec((ts, C), lambda b, s: (b * ns + s, 0)),
            pl.BlockSpec((G, cg, cg), lambda b, s: (0, 0, 0)),
            pl.BlockSpec((1, C), lambda b, s: (0, 0)),
        ],
        out_specs=pl.BlockSpec((ts, C), lambda b, s: (b * ns + s, 0)),
        out_shape=jax.ShapeDtypeStruct((T, C), BF16),
        scratch_shapes=[pltpu.VMEM((ts + MAX_WINDOW, C), F32)],
        compiler_params=_params(("arbitrary", "arbitrary")),
        name="pool",
    )(u, w_pool, pool_scale)


def _attn_kernel(qT_ref, iqT_ref, iwT_ref, kr_ref, ikp_ref, vT_ref, o_ref,
                 key_ref, hi_ref, lo_ref, vte_ref, cut_ref, qg_ref, m_ref, acc_ref,
                 s0_ref, s1_ref, mt0_ref, mt1_ref, *, seq, topk):
    i = pl.program_id(1)
    n_tiles = (i * TQ + TQ + TK - 1) // TK
    row_bits = (seq - 1).bit_length()

    @pl.when(i == 0)
    def _():
        for n in range(N_KV_HEADS):
            vte_ref[n, 0:HEAD_DIM, :] = vT_ref[n * HEAD_DIM:(n + 1) * HEAD_DIM, :]
            vte_ref[n, HEAD_DIM:V_ROWS, :] = jnp.ones((V_ROWS - HEAD_DIM, seq), BF16)

    qpos = i * TQ + lax.broadcasted_iota(I32, (1, TQ), 1)

    def tile_rows(t):
        r0 = pl.multiple_of(t * TK, TK)
        return r0, r0 + lax.broadcasted_iota(I32, (TK, TQ), 0)

    iq = iqT_ref[...]
    idx_rhs = jnp.concatenate(
        [iq[h * IDX_DIM:(h + 1) * IDX_DIM, :] for h in range(IDX_HEADS)], axis=1)
    idx_rhs = jnp.concatenate([idx_rhs, jnp.zeros_like(idx_rhs)], axis=0)
    iw = iwT_ref[...]

    def score_tile(t):
        r0, rows = tile_rows(t)
        logits = jnp.dot(ikp_ref[pl.ds(r0, TK), :], idx_rhs, preferred_element_type=F32)
        sc = jnp.zeros((TK, TQ), F32)
        for h in range(IDX_HEADS):
            sc = sc + jnp.maximum(logits[:, h * TQ:(h + 1) * TQ], 0.0) * iw[h:h + 1, :]
        sc = jnp.where(rows <= qpos, sc + 0.0, -jnp.inf)
        bits = pltpu.bitcast(sc, I32)
        key = jnp.where(bits < 0, bits ^ 0x7FFFFFFF, bits)
        key_ref[pl.ds(r0, TK), :] = key
        hi_ref[pl.ds(r0, TK), :] = (key >> 16).astype(I16)
        lo_ref[pl.ds(r0, TK), :] = (((key ^ 0x8000) << 16) >> 16).astype(I16)

    n_pairs = (n_tiles + 1) // 2

    def score_body(tt, carry):
        score_tile(2 * tt)
        score_tile(2 * tt + 1)
        return carry

    lax.fori_loop(0, n_pairs, score_body, 0)

    def rows16(v):
        return jnp.broadcast_to(v, (PACK, TQ)).astype(I16)

    def count16(ref, cand, op):
        c16 = rows16(cand)

        def body(t, accs):
            v = ref[pl.ds(pl.multiple_of(t * 2 * TK, 2 * TK), 2 * TK), :]
            accs = list(accs)
            for r in range(2 * TK // PACK):
                a = r % len(accs)
                hit = op(v[r * PACK:(r + 1) * PACK, :], c16)
                accs[a] = accs[a] + jnp.where(hit, jnp.int16(1), jnp.int16(0))
            return tuple(accs)

        zero = jnp.zeros((PACK, TQ), I16)
        accs = lax.fori_loop(0, n_pairs, body, (zero,) * 4)
        total = sum(a.astype(I32) for a in accs)
        return jnp.sum(total, axis=0, keepdims=True)

    ge = lambda v, c: v >= c
    gt = lambda v, c: v > c

    def half_search(ref, base, count_all):
        def bit(p, carry):
            lo, cnt_lo = carry
            cand = lo + jnp.left_shift(jnp.int32(1), 15 - p)
            cnt = base + count16(ref, cand, ge)
            ok = cnt >= topk
            return jnp.where(ok, cand, lo), jnp.where(ok, cnt, cnt_lo)
        return lax.fori_loop(0, 16, bit, (jnp.full((1, TQ), I16_MIN, I32), count_all))

    tau_hi, c_ge_hi = half_search(hi_ref, 0, jnp.broadcast_to(n_pairs * 2 * TK, (1, TQ)))
    c_gt_hi = count16(hi_ref, tau_hi, gt)
    tau_hi16 = rows16(tau_hi)

    def narrow_body(t, carry):
        sl = pl.ds(pl.multiple_of(t * 2 * TK, 2 * TK), 2 * TK)
        hi, lo = hi_ref[sl, :], lo_ref[sl, :]
        lo_ref[sl, :] = jnp.concatenate(
            [jnp.where(hi[r * PACK:(r + 1) * PACK, :] == tau_hi16,
                       lo[r * PACK:(r + 1) * PACK, :], jnp.int16(I16_MIN))
             for r in range(2 * TK // PACK)], axis=0)
        return carry

    lax.fori_loop(0, n_pairs, narrow_body, 0)
    tau_lo, c_ge = half_search(lo_ref, c_gt_hi, c_ge_hi)
    tau = tau_hi * 65536 + (tau_lo - I16_MIN)

    def count(pred):
        def body(t, c8):
            r0, rows = tile_rows(t)
            m = pred(key_ref[pl.ds(r0, TK), :], rows).astype(I32)
            return c8 + jnp.sum(m.reshape(TK // 8, 8, TQ), axis=0)
        c8 = lax.fori_loop(0, n_tiles, body, jnp.zeros((8, TQ), I32))
        return jnp.sum(c8, axis=0, keepdims=True)

    cut_ref[...] = jnp.full((1, TQ), seq, I32)
    tied = jnp.logical_and(c_ge > topk, tau > NEG_INF_KEY)

    @pl.when(jnp.max(tied.astype(I32)) > 0)
    def _():
        need = topk - count(lambda k, rows: k > tau)

        def row_bit(p, x):
            cand = x + jnp.left_shift(jnp.int32(1), row_bits - 1 - p)
            cnt = count(lambda k, rows: jnp.logical_and(k == tau, rows < cand))
            return jnp.where(cnt < need, cand, x)
        cut_ref[...] = lax.fori_loop(0, row_bits, row_bit, jnp.zeros((1, TQ), I32))

    cut = cut_ref[...]

    def mask_body(t, carry):
        r0, rows = tile_rows(t)
        k = key_ref[pl.ds(r0, TK), :]
        at_cut = jnp.where(rows <= cut, 0, MASK_NEG_BITS)
        keep = jnp.where(k > tau, 0, jnp.where(k == tau, at_cut, MASK_NEG_BITS))
        key_ref[pl.ds(r0, TK), :] = jnp.where(rows <= qpos, keep, MASK_NEG_BITS).astype(I32)
        return carry

    lax.fori_loop(0, n_tiles, mask_body, 0)

    for n in range(N_KV_HEADS):
        qg_ref[n] = jnp.concatenate(
            [qT_ref[(GROUP * n + g) * HEAD_DIM:(GROUP * n + g + 1) * HEAD_DIM, :]
             for g in range(GROUP)], axis=1)
        m_ref[n] = jnp.full((1, GROUP * TQ), MASK_NEG, F32)
        acc_ref[n] = jnp.zeros((V_ROWS, GROUP * TQ), F32)

    def score_group(t, n):
        r0 = pl.multiple_of(jnp.minimum(t, n_tiles - 1) * TK, TK)
        bias = pltpu.bitcast(key_ref[pl.ds(r0, TK), :], F32)
        bias = jnp.concatenate([bias] * GROUP, axis=1)
        s = jnp.dot(kr_ref[pl.ds(r0, TK), n * HEAD_DIM:(n + 1) * HEAD_DIM], qg_ref[n],
                    preferred_element_type=F32) + bias
        s0_ref[n] = s
        mt0_ref[n] = jnp.max(s, axis=0, keepdims=True)

    def value_group(t, n):
        r0 = pl.multiple_of(t * TK, TK)
        m = m_ref[n]
        m_new = jnp.maximum(m, mt0_ref[n])
        p = jnp.exp2(s0_ref[n] - m_new).astype(BF16)
        pv = jnp.dot(vte_ref[n, :, pl.ds(r0, TK)], p, preferred_element_type=F32)
        acc_ref[n] = acc_ref[n] * jnp.exp2(m - m_new) + pv
        m_ref[n] = m_new

    score_group(0, 0)

    def attn_body(t, carry):
        for n in range(N_KV_HEADS):
            if n + 1 < N_KV_HEADS:
                score_group(t, n + 1)
            value_group(t, n)
            if n + 1 == N_KV_HEADS:
                score_group(t + 1, 0)
        return carry

    lax.fori_loop(0, n_tiles, attn_body, 0)

    for n in range(N_KV_HEADS):
        acc = acc_ref[n]
        o = acc[0:HEAD_DIM, :] / acc[HEAD_DIM:HEAD_DIM + 1, :]
        for g in range(GROUP):
            c0 = (GROUP * n + g) * HEAD_DIM
            o_ref[:, c0:c0 + HEAD_DIM] = o[:, g * TQ:(g + 1) * TQ].T.astype(BF16)


def _attention(qT, iqT, iwT, kr, ikp, vT, batch, seq):
    T = kr.shape[0]
    nq = seq // TQ
    topk = min(TOPK_MAX, seq // 4)
    assert seq % (2 * TK) == 0 and seq % TQ == 0
    once = pl.Buffered(1)
    return pl.pallas_call(
        functools.partial(_attn_kernel, seq=seq, topk=topk),
        grid=(batch, nq),
        in_specs=[
            pl.BlockSpec((ATTN_WIDTH, TQ), lambda b, i: (0, b * nq + i)),
            pl.BlockSpec((IDX_WIDTH, TQ), lambda b, i: (0, b * nq + i)),
            pl.BlockSpec((LANES, TQ), lambda b, i: (0, b * nq + i)),
            pl.BlockSpec((seq, KV_WIDTH), lambda b, i: (b, 0), pipeline_mode=once),
            pl.BlockSpec((seq, LANES), lambda b, i: (b, 0), pipeline_mode=once),
            pl.BlockSpec((KV_WIDTH, seq), lambda b, i: (0, b), pipeline_mode=once),
        ],
        out_specs=pl.BlockSpec((TQ, ATTN_WIDTH), lambda b, i: (b * nq + i, 0)),
        out_shape=jax.ShapeDtypeStruct((T, ATTN_WIDTH), BF16),
        scratch_shapes=[
            pltpu.VMEM((seq, TQ), I32),
            pltpu.VMEM((seq, TQ), I16),
            pltpu.VMEM((seq, TQ), I16),
            pltpu.VMEM((N_KV_HEADS, V_ROWS, seq), BF16),
            pltpu.VMEM((1, TQ), I32),
            pltpu.VMEM((N_KV_HEADS, HEAD_DIM, GROUP * TQ), BF16),
            pltpu.VMEM((N_KV_HEADS, 1, GROUP * TQ), F32),
            pltpu.VMEM((N_KV_HEADS, V_ROWS, GROUP * TQ), F32),
            pltpu.VMEM((N_KV_HEADS, TK, GROUP * TQ), F32),
            pltpu.VMEM((N_KV_HEADS, TK, GROUP * TQ), F32),
            pltpu.VMEM((N_KV_HEADS, 1, GROUP * TQ), F32),
            pltpu.VMEM((N_KV_HEADS, 1, GROUP * TQ), F32),
        ],
        compiler_params=_params(("arbitrary", "arbitrary")),
        name="sparse_attn",
    )(qT, iqT, iwT, kr, ikp, vT)


def _outproj_kernel(a_ref, p_ref, wa_ref, wp_ref, x_ref, o_ref):
    o_ref[...] = (x_ref[...]
                  + jnp.dot(a_ref[...], wa_ref[...], preferred_element_type=F32)
                  + jnp.dot(p_ref[...], wp_ref[...], preferred_element_type=F32))


def _outproj(attn, pool, w_out, x2):
    T, D = x2.shape
    tm, tn = 512, 1024
    ka, kp = attn.shape[1], pool.shape[1]
    assert ka == kp
    return pl.pallas_call(
        _outproj_kernel,
        grid=(T // tm, D // tn),
        in_specs=[
            pl.BlockSpec((tm, ka), lambda i, j: (i, 0)),
            pl.BlockSpec((tm, kp), lambda i, j: (i, 0)),
            pl.BlockSpec((ka, tn), lambda i, j: (0, j)),
            pl.BlockSpec((kp, tn), lambda i, j: (1, j)),
            pl.BlockSpec((tm, tn), lambda i, j: (i, j)),
        ],
        out_specs=pl.BlockSpec((tm, tn), lambda i, j: (i, j)),
        out_shape=jax.ShapeDtypeStruct((T, D), F32),
        compiler_params=_params(("parallel", "parallel")),
        name="outproj",
    )(attn, pool, w_out, w_out, x2)


def _gateup_kernel(x_ref, g_ref, wg_ref, wu_ref, a_ref, h_ref):
    @pl.when(pl.program_id(1) == 0)
    def _():
        h_ref[...] = _rmsnorm_bf16(x_ref[...], g_ref[...])

    h = h_ref[...]
    gate = jnp.dot(h, wg_ref[...], preferred_element_type=F32)
    up = jnp.dot(h, wu_ref[...], preferred_element_type=F32)
    a_ref[...] = (gate * (1.0 / (1.0 + jnp.exp(-gate))) * up).astype(BF16)


def _gateup(x1, g, w_gate, w_up):
    T, D = x1.shape
    F = w_gate.shape[1]
    tm, tn = 512, 512
    return pl.pallas_call(
        _gateup_kernel,
        grid=(T // tm, pl.cdiv(F, tn)),
        in_specs=[
            pl.BlockSpec((tm, D), lambda i, j: (i, 0)),
            pl.BlockSpec((1, D), lambda i, j: (0, 0)),
            pl.BlockSpec((D, tn), lambda i, j: (0, j)),
            pl.BlockSpec((D, tn), lambda i, j: (0, j)),
        ],
        out_specs=pl.BlockSpec((tm, tn), lambda i, j: (i, j)),
        out_shape=jax.ShapeDtypeStruct((T, F), BF16),
        scratch_shapes=[pltpu.VMEM((tm, D), BF16)],
        compiler_params=_params(("arbitrary", "arbitrary")),
        name="ffn_gateup",
    )(x1, g, w_gate, w_up)


def _down_kernel(a_ref, w_ref, x_ref, o_ref, acc_ref):
    k = pl.program_id(2)

    @pl.when(k == 0)
    def _():
        acc_ref[...] = x_ref[...]

    acc_ref[...] += jnp.dot(a_ref[...], w_ref[...], preferred_element_type=F32)

    @pl.when(k == pl.num_programs(2) - 1)
    def _():
        o_ref[...] = acc_ref[...]


def _down(a, w_down, x1):
    T, D = x1.shape
    F = a.shape[1]
    tm, tn = 512, 1024
    tk = F // 2 if (F // 2) % LANES == 0 else F
    return pl.pallas_call(
        _down_kernel,
        grid=(T // tm, D // tn, F // tk),
        in_specs=[
            pl.BlockSpec((tm, tk), lambda i, j, k: (i, k)),
            pl.BlockSpec((tk, tn), lambda i, j, k: (k, j)),
            pl.BlockSpec((tm, tn), lambda i, j, k: (i, j)),
        ],
        out_specs=pl.BlockSpec((tm, tn), lambda i, j, k: (i, j)),
        out_shape=jax.ShapeDtypeStruct((T, D), F32),
        scratch_shapes=[pltpu.VMEM((tm, tn), F32)],
        compiler_params=_params(("parallel", "parallel", "arbitrary")),
        name="ffn_down",
    )(a, w_down, x1)


def _final_norm_kernel(x_ref, g_ref, o_ref):
    x = x_ref[...]
    ms = jnp.mean(x * x, axis=-1, keepdims=True)
    o_ref[...] = x * lax.rsqrt(ms + EPS) * g_ref[...]


def _final_norm(x2, g):
    T, D = x2.shape
    tm = 256
    return pl.pallas_call(
        _final_norm_kernel,
        grid=(T // tm,),
        in_specs=[pl.BlockSpec((tm, D), lambda i: (i, 0)),
                  pl.BlockSpec((1, D), lambda i: (0, 0))],
        out_specs=pl.BlockSpec((tm, D), lambda i: (i, 0)),
        out_shape=jax.ShapeDtypeStruct((T, D), F32),
        compiler_params=_params(("parallel",)),
        name="final_norm",
    )(x2, g)


def _rope_table(seq, head_dim):
    rot = head_dim // ROT_FRACTION
    half = rot // 2
    inv_freq = ROPE_THETA ** (-jnp.arange(0, rot, 2, dtype=F32) / rot)
    ang = jnp.arange(seq).astype(F32)[:, None] * inv_freq[None, :]
    cos, sin = jnp.cos(ang), jnp.sin(ang)
    feature_major = jnp.stack([cos.T, sin.T])
    zeros = lambda n: jnp.zeros((seq, n), F32)
    c = jnp.concatenate([cos, cos, jnp.ones((seq, head_dim - rot), F32)], axis=1)
    sa = jnp.concatenate([-sin, zeros(head_dim - half)], axis=1)
    sb = jnp.concatenate([zeros(half), sin, zeros(head_dim - rot)], axis=1)
    reps = LANES // head_dim
    return jnp.stack([jnp.tile(t, (1, reps)) for t in (c, sa, sb)]), feature_major


def kernel(x, norm_mix_g, w_in, w_pool, pool_scale, w_out, norm_ffn_g, w_gate, w_up, w_down,
           norm_final_g):
    B, S, D = x.shape
    T = B * S
    depth = w_in.shape[0]
    pool_w = pool_scale.shape[1]
    small0 = ATTN_WIDTH + 2 * KV_WIDTH + IDX_WIDTH
    small1 = small0 + IDX_DIM + IDX_HEADS

    rope_a, rope_ta = _rope_table(S, HEAD_DIM)
    rope_i, rope_ti = _rope_table(S, IDX_DIM)
    x2 = x.reshape(T, D)
    for l in range(depth):
        wl = w_in[l].astype(BF16)
        w_main = jnp.concatenate([wl[:, :small0], wl[:, small1:]], axis=1)
        w_small = jnp.concatenate(
            [wl[:, small0:small0 + IDX_DIM], jnp.zeros((D, LANES - IDX_DIM), BF16),
             wl[:, small0 + IDX_DIM:small1], jnp.zeros((D, LANES - IDX_HEADS), BF16)],
            axis=1)

        qT, kr, vT, iqT, ikp, iwT, u = _inproj(
            x2, norm_mix_g[l][None], w_main, w_small, rope_a, rope_i, rope_ta, rope_ti, S)
        pool = _pool(u, w_pool[l].astype(BF16), pool_scale[l][None], B, S)
        attn = _attention(qT, iqT, iwT, kr, ikp, vT, B, S)
        x2 = _outproj(attn, pool, w_out[l].astype(BF16), x2)
        a = _gateup(x2, norm_ffn_g[l][None], w_gate[l].astype(BF16), w_up[l].astype(BF16))
        x2 = _down(a, w_down[l].astype(BF16), x2)
    return _final_norm(x2, norm_final_g[None]).reshape(B, S, D)
```
